```python
import math
import jax, jax.numpy as jnp
from jax import lax
import numpy as np

D_MODEL = 1024
BATCH = 2
SEQ = 8192
DEPTH = 1

HEAD_DIM = 64
A_Q_HEADS = 8
A_KV_HEADS = 2
A_GROUP = A_Q_HEADS // A_KV_HEADS
B_HEADS = 4
B_V_DIM = 2 * HEAD_DIM
A_WIDTH = A_Q_HEADS * HEAD_DIM
B_WIDTH = B_HEADS * B_V_DIM
GRID_W = 64
ROPE_THETA = 10000.0
Q_BLOCK = 128
EPS = 1e-6
LN_EPS = 1e-5
PEER_HEADS = 8
PEER_N_KEYS = 128
PEER_N_EXPERTS = PEER_N_KEYS * PEER_N_KEYS
PEER_QUERY_DIM = 256
PEER_HALF = PEER_QUERY_DIM // 2
PEER_TOPK = 16
PEER_TOKEN_BLOCK = 128
DN_ALPHA = (2.0 * DEPTH) ** 0.25
DN_BETA = (8.0 * DEPTH) ** -0.25

COLS_A_Q = A_Q_HEADS * HEAD_DIM
COLS_A_KV = A_KV_HEADS * HEAD_DIM
COLS_B_QK = B_HEADS * 2 * HEAD_DIM
COLS_B_V = B_HEADS * B_V_DIM
COLS_GATES = 2 * D_MODEL
IN_COLS = COLS_A_Q + 2 * COLS_A_KV + 2 * COLS_B_QK + COLS_B_V + COLS_GATES
SPLIT_1 = COLS_A_Q
SPLIT_2 = SPLIT_1 + COLS_A_KV
SPLIT_3 = SPLIT_2 + COLS_A_KV
SPLIT_4 = SPLIT_3 + COLS_B_QK
SPLIT_5 = SPLIT_4 + COLS_B_QK
SPLIT_6 = SPLIT_5 + COLS_B_V

kernel_name = 'hybrid_gqa_diffattn_peer_deepnorm'


def rms_norm(x, g):
    xf = x.astype(jnp.float32)
    y = xf * lax.rsqrt(jnp.mean(xf * xf, axis=-1, keepdims=True) + EPS)
    return (y * g.astype(jnp.float32)).astype(x.dtype)


def layer_norm(x, g, b):
    xf = x.astype(jnp.float32)
    mu = jnp.mean(xf, axis=-1, keepdims=True)
    xc = xf - mu
    var = jnp.mean(xc * xc, axis=-1, keepdims=True)
    y = xc * lax.rsqrt(var + LN_EPS) * g.astype(jnp.float32) + b.astype(jnp.float32)
    return y.astype(x.dtype)


def rope_tables(pos, dim):
    freqs = ROPE_THETA ** (-jnp.arange(0, dim, 2, dtype=jnp.float32) / dim)
    ang = pos.astype(jnp.float32)[:, None] * freqs[None, :]
    return jnp.cos(ang), jnp.sin(ang)


def apply_rope(x, cos, sin):
    xf = x.astype(jnp.float32)
    half = x.shape[-1] // 2
    x1, x2 = xf[..., :half], xf[..., half:]
    c = cos[None, :, None, :]
    s = sin[None, :, None, :]
    return jnp.concatenate([x1 * c - x2 * s, x2 * c + x1 * s], axis=-1).astype(x.dtype)


def apply_axial_rope(x, row_tab, col_tab):
    half = x.shape[-1] // 2
    return jnp.concatenate([apply_rope(x[..., :half], *row_tab),
                            apply_rope(x[..., half:], *col_tab)], axis=-1)


def to_query_blocks(q):
    shape = q.shape
    nb = shape[-2] // Q_BLOCK
    qb = q.reshape(shape[:-2] + (nb, Q_BLOCK, shape[-1]))
    return jnp.moveaxis(qb, -3, 0)


def from_query_blocks(ob):
    ob = jnp.moveaxis(ob, 0, -3)
    shape = ob.shape
    return ob.reshape(shape[:-3] + (shape[-3] * shape[-2], shape[-1]))


def gqa_attention(q, k, v):
    B, S, _, d = q.shape
    qg = q.reshape(B, S, A_KV_HEADS, A_GROUP, d).transpose(0, 2, 3, 1, 4)
    kg = k.transpose(0, 2, 1, 3)
    vg = v.transpose(0, 2, 1, 3)
    scale = d ** -0.5

    def block(qb):
        s = jnp.einsum('bgrqd,bgkd->bgrqk', qb, kg).astype(jnp.float32) * scale
        p = jax.nn.softmax(s, axis=-1)
        return jnp.einsum('bgrqk,bgkd->bgrqd', p.astype(vg.dtype), vg)

    o = from_query_blocks(lax.map(block, to_query_blocks(qg)))
    return o.transpose(0, 3, 1, 2, 4).reshape(B, S, A_WIDTH)


def diff_attention(q, k, v, lam, subln_g, lambda_init):
    B, S, _, _, d = q.shape
    qh = q.transpose(0, 2, 3, 1, 4)
    kh = k.transpose(0, 2, 3, 1, 4)
    vh = v.transpose(0, 2, 1, 3)
    scale = d ** -0.5

    def block(qb):
        s = jnp.einsum('bhiqd,bhikd->bhiqk', qb, kh).astype(jnp.float32) * scale
        p = jax.nn.softmax(s, axis=-1)
        a = p[:, :, 0] - lam * p[:, :, 1]
        return jnp.einsum('bhqk,bhkv->bhqv', a.astype(vh.dtype), vh)

    o = from_query_blocks(lax.map(block, to_query_blocks(qh)))
    o = rms_norm(o, subln_g) * (1.0 - lambda_init)
    return o.transpose(0, 2, 1, 3).reshape(B, S, B_WIDTH)


def peer_ffn(h, w_query, sub_keys, u_table, v_table):
    B, S, D = h.shape
    T = B * S
    hf = h.reshape(T, D)
    q = (hf @ w_query).reshape(T, PEER_HEADS, 2, PEER_HALF)
    s = jnp.einsum('thid,hind->thin', q, sub_keys).astype(jnp.float32)
    sv, si = lax.top_k(s, PEER_TOPK)
    cand = (sv[:, :, 0, :, None] + sv[:, :, 1, None, :]).reshape(T, PEER_HEADS, PEER_TOPK * PEER_TOPK)
    cv, ci = lax.top_k(cand, PEER_TOPK)
    i1 = jnp.take_along_axis(si[:, :, 0, :], ci // PEER_TOPK, axis=-1)
    i2 = jnp.take_along_axis(si[:, :, 1, :], ci % PEER_TOPK, axis=-1)
    experts = i1 * PEER_N_KEYS + i2
    gates = jax.nn.softmax(cv, axis=-1)

    nbt = T // PEER_TOKEN_BLOCK
    hb_all = hf.reshape(nbt, PEER_TOKEN_BLOCK, D)
    e_all = experts.reshape(nbt, PEER_TOKEN_BLOCK, PEER_HEADS, PEER_TOPK)
    g_all = gates.reshape(nbt, PEER_TOKEN_BLOCK, PEER_HEADS, PEER_TOPK)

    def block(args):
        hb, eb, gb = args
        u = jnp.take(u_table, eb, axis=0)
        a = jnp.einsum('td,thkd->thk', hb, u).astype(jnp.float32)
        w = jax.nn.gelu(a, approximate=False) * gb
        vv = jnp.take(v_table, eb, axis=0)
        return jnp.einsum('thk,thkd->td', w.astype(hb.dtype), vv)

    y = lax.map(block, (hb_all, e_all, g_all))
    return y.reshape(B, S, D)


def hybrid_layer(x, w_in, b_gate, q_norm_a, k_norm_a, lambda_q1, lambda_k1, lambda_q2, lambda_k2,
                 subln_b, w_proj_a, w_proj_b, w_out, ln1_g, ln1_b, w_query, sub_keys,
                 u_table, v_table, ln2_g, ln2_b, lambda_init, row_tab, col_tab, t_tab):
    B, S, D = x.shape
    proj = x @ w_in
    qa, ka, va, qb, kb, vb, gate_logits = jnp.split(
        proj, [SPLIT_1, SPLIT_2, SPLIT_3, SPLIT_4, SPLIT_5, SPLIT_6], axis=-1)

    qa = apply_axial_rope(rms_norm(qa.reshape(B, S, A_Q_HEADS, HEAD_DIM), q_norm_a), row_tab, col_tab)
    ka = apply_axial_rope(rms_norm(ka.reshape(B, S, A_KV_HEADS, HEAD_DIM), k_norm_a), row_tab, col_tab)
    va = va.reshape(B, S, A_KV_HEADS, HEAD_DIM)
    ya = gqa_attention(qa, ka, va)

    qb = apply_rope(qb.reshape(B, S, B_HEADS * 2, HEAD_DIM), *t_tab).reshape(B, S, B_HEADS, 2, HEAD_DIM)
    kb = apply_rope(kb.reshape(B, S, B_HEADS * 2, HEAD_DIM), *t_tab).reshape(B, S, B_HEADS, 2, HEAD_DIM)
    vb = vb.reshape(B, S, B_HEADS, B_V_DIM)
    lam = (jnp.exp(jnp.sum(lambda_q1.astype(jnp.float32) * lambda_k1.astype(jnp.float32)))
           - jnp.exp(jnp.sum(lambda_q2.astype(jnp.float32) * lambda_k2.astype(jnp.float32)))
           + lambda_init)
    yb = diff_attention(qb, kb, vb, lam, subln_b, lambda_init)

    gates = jax.nn.sigmoid(gate_logits + b_gate).reshape(B, S, 2, D)
    merged = gates[:, :, 0, :] * (ya @ w_proj_a) + gates[:, :, 1, :] * (yb @ w_proj_b)
    h = layer_norm(DN_ALPHA * x + merged @ w_out, ln1_g, ln1_b)

    y = peer_ffn(h, w_query, sub_keys, u_table, v_table)
    return layer_norm(DN_ALPHA * h + y, ln2_g, ln2_b)


def setup_inputs(seed: int = 0) -> dict:
    key = jax.random.key(seed)
    ks = jax.random.split(key, 21)
    L, D = DEPTH, D_MODEL

    def nrm(k, shape, scale):
        return jax.random.normal(k, shape, jnp.float32) * scale

    return {
        'x': nrm(ks[0], (BATCH, SEQ, D), 1.0),
        'w_in': nrm(ks[1], (L, D, IN_COLS), D ** -0.5),
        'b_gate': nrm(ks[2], (L, COLS_GATES), 0.01),
        'q_norm_a': 1.0 + nrm(ks[3], (L, HEAD_DIM), 0.02),
        'k_norm_a': 1.0 + nrm(ks[4], (L, HEAD_DIM), 0.02),
        'lambda_q1': nrm(ks[5], (L, HEAD_DIM), 0.1),
        'lambda_k1': nrm(ks[6], (L, HEAD_DIM), 0.1),
        'lambda_q2': nrm(ks[7], (L, HEAD_DIM), 0.1),
        'lambda_k2': nrm(ks[8], (L, HEAD_DIM), 0.1),
        'subln_b': 1.0 + nrm(ks[9], (L, B_V_DIM), 0.02),
        'w_proj_a': nrm(ks[10], (L, A_WIDTH, D), A_WIDTH ** -0.5 * DN_BETA),
        'w_proj_b': nrm(ks[11], (L, B_WIDTH, D), B_WIDTH ** -0.5 * DN_BETA),
        'w_out': nrm(ks[12], (L, D, D), D ** -0.5 * DN_BETA),
        'ln1_g': 1.0 + nrm(ks[13], (L, D), 0.02),
        'ln1_b': nrm(ks[14], (L, D), 0.01),
        'w_query': nrm(ks[15], (L, D, PEER_HEADS * PEER_QUERY_DIM), D ** -0.5),
        'sub_keys': nrm(ks[16], (L, PEER_HEADS, 2, PEER_N_KEYS, PEER_HALF), PEER_HALF ** -0.5),
        'u_table': nrm(ks[17], (L, PEER_N_EXPERTS, D), D ** -0.5),
        'v_table': nrm(ks[18], (L, PEER_N_EXPERTS, D), DN_BETA),
        'ln2_g': 1.0 + nrm(ks[19], (L, D), 0.02),
        'ln2_b': nrm(ks[20], (L, D), 0.01),
    }


def reference(x, w_in, b_gate, q_norm_a, k_norm_a, lambda_q1, lambda_k1, lambda_q2, lambda_k2,
              subln_b, w_proj_a, w_proj_b, w_out, ln1_g, ln1_b, w_query, sub_keys,
              u_table, v_table, ln2_g, ln2_b):
    S = x.shape[1]
    rows = S // GRID_W
    t_pos = jnp.arange(S, dtype=jnp.int32)
    row_pos = jnp.repeat(jnp.arange(rows, dtype=jnp.int32), GRID_W)
    col_pos = jnp.tile(jnp.arange(GRID_W, dtype=jnp.int32), rows)
    row_tab = rope_tables(row_pos, HEAD_DIM // 2)
    col_tab = rope_tables(col_pos, HEAD_DIM // 2)
    t_tab = rope_tables(t_pos, HEAD_DIM)

    h = x
    for l in range(DEPTH):
        lambda_init = 0.8 - 0.6 * math.exp(-0.3 * l)
        h = hybrid_layer(h, w_in[l], b_gate[l], q_norm_a[l], k_norm_a[l], lambda_q1[l], lambda_k1[l],
                         lambda_q2[l], lambda_k2[l], subln_b[l], w_proj_a[l], w_proj_b[l], w_out[l],
                         ln1_g[l], ln1_b[l], w_query[l], sub_keys[l], u_table[l], v_table[l],
                         ln2_g[l], ln2_b[l], lambda_init, row_tab, col_tab, t_tab)
    return h
```

```python
import functools
import math

import jax
import jax.numpy as jnp
from jax import lax
from jax.experimental import pallas as pl
from jax.experimental.pallas import tpu as pltpu

F32 = jnp.float32
BF16 = jnp.bfloat16

D_MODEL = 1024
HEAD_DIM = 64
A_Q_HEADS = 8
A_KV_HEADS = 2
A_GROUP = A_Q_HEADS // A_KV_HEADS
B_HEADS = 4
B_V_DIM = 2 * HEAD_DIM
GRID_W = 64
ROPE_THETA = 10000.0
EPS = 1e-6
LN_EPS = 1e-5
PEER_HEADS = 8
PEER_N_KEYS = 128
PEER_N_EXPERTS = PEER_N_KEYS * PEER_N_KEYS
PEER_HALF = 128
PEER_TOPK = 16
DEPTH = 1
DN_ALPHA = (2.0 * DEPTH) ** 0.25
LAMBDA_INIT = 0.8 - 0.6 * math.exp(-0.3 * 0)

COLS_A_Q = A_Q_HEADS * HEAD_DIM
COLS_A_KV = A_KV_HEADS * HEAD_DIM
COLS_B_QK = B_HEADS * 2 * HEAD_DIM
COLS_B_V = B_HEADS * B_V_DIM
COLS_QKV = COLS_A_Q + 2 * COLS_A_KV + 2 * COLS_B_QK + COLS_B_V

LANES = 128
VMEM_LIMIT = 48 * 1024 * 1024

_CAND_PAIRS = tuple((k, l) for k in range(PEER_TOPK) for l in range(PEER_TOPK)
                    if (k + 1) * (l + 1) <= PEER_TOPK + 1)


def _cparams(sem):
    return pltpu.CompilerParams(dimension_semantics=sem, vmem_limit_bytes=VMEM_LIMIT)


def _rope_tables(seq):
    t = jnp.arange(seq, dtype=jnp.int32)

    def tab(pos, dim):
        freqs = ROPE_THETA ** (-jnp.arange(0, dim, 2, dtype=F32) / dim)
        ang = pos.astype(F32)[:, None] * freqs[None, :]
        return jnp.cos(ang), jnp.sin(ang)

    cr, sr = tab(t // GRID_W, HEAD_DIM // 2)
    cc, sc = tab(t % GRID_W, HEAD_DIM // 2)
    ct, st = tab(t, HEAD_DIM)
    z16 = jnp.zeros_like(sr)
    z32 = jnp.zeros_like(st)
    cos_a = jnp.concatenate([cr, cr, cc, cc], axis=-1)
    up_a = jnp.concatenate([-sr, z16, -sc, z16], axis=-1)
    dn_a = jnp.concatenate([z16, sr, z16, sc], axis=-1)
    cos_b = jnp.concatenate([ct, ct], axis=-1)
    up_b = jnp.concatenate([-st, z32], axis=-1)
    dn_b = jnp.concatenate([z32, st], axis=-1)
    two = lambda a: jnp.concatenate([a, a], axis=-1)
    return tuple(two(a) for a in (cos_a, up_a, dn_a, cos_b, up_b, dn_b))


def _rotate(x, cos, up, dn, half):
    n = x.shape[-1]
    return x * cos + pltpu.roll(x, n - half, 1) * up + pltpu.roll(x, half, 1) * dn


def _prep_kernel(x_ref, w_ref, gq_ref, gk_ref, ca_ref, ua_ref, da_ref, cb_ref, ub_ref, db_ref,
                 gsum_ref, qa_ref, ka_ref, va_ref, qb_ref, kb_ref, vb_ref):
    xb = x_ref[0].astype(BF16)
    tm = xb.shape[0]
    lane = lax.broadcasted_iota(jnp.int32, (tm, LANES), 1)
    ones_col = (lane == HEAD_DIM).astype(F32)
    scale = HEAD_DIM ** -0.5

    def proj(c0, width):
        return jnp.dot(xb, w_ref[:, c0:c0 + width], preferred_element_type=F32)

    def norm_rope_a(y, gain):
        ss = jnp.dot((y * y).astype(BF16), gsum_ref[...], preferred_element_type=F32)
        y = y * lax.rsqrt(ss * (1.0 / HEAD_DIM) + EPS) * gain
        return _rotate(y, ca_ref[...], ua_ref[...], da_ref[...], HEAD_DIM // 4)

    def rope_b(y):
        return _rotate(y, cb_ref[...], ub_ref[...], db_ref[...], HEAD_DIM // 2)

    def store_heads(ref, y, c, mul):
        y = (y * mul).astype(ref.dtype) if mul != 1.0 else y.astype(ref.dtype)
        ref[0, 2 * c] = y[:, :HEAD_DIM]
        ref[0, 2 * c + 1] = y[:, HEAD_DIM:]

    c0 = 0
    qa = proj(c0, COLS_A_Q)
    for c in range(COLS_A_Q // LANES):
        y = norm_rope_a(qa[:, c * LANES:(c + 1) * LANES], gq_ref[...])
        store_heads(qa_ref, y, c, scale)
    c0 += COLS_A_Q
    ka = proj(c0, COLS_A_KV)
    store_heads(ka_ref, norm_rope_a(ka, gk_ref[...]), 0, 1.0)
    c0 += COLS_A_KV
    va = proj(c0, COLS_A_KV)
    va_ref[0, 0] = jnp.where(lane < HEAD_DIM, va, ones_col).astype(va_ref.dtype)
    va_ref[0, 1] = jnp.where(lane < HEAD_DIM, pltpu.roll(va, HEAD_DIM, 1), ones_col).astype(va_ref.dtype)
    c0 += COLS_A_KV
    qb = proj(c0, COLS_B_QK)
    for c in range(COLS_B_QK // LANES):
        store_heads(qb_ref, rope_b(qb[:, c * LANES:(c + 1) * LANES]), c, scale)
    c0 += COLS_B_QK
    kb = proj(c0, COLS_B_QK)
    for c in range(COLS_B_QK // LANES):
        store_heads(kb_ref, rope_b(kb[:, c * LANES:(c + 1) * LANES]), c, 1.0)
    c0 += COLS_B_QK
    vb = proj(c0, COLS_B_V)
    first_col = (lane == 0).astype(vb_ref.dtype)
    for h in range(B_HEADS):
        vb_ref[0, h, :, :B_V_DIM] = vb[:, h * B_V_DIM:(h + 1) * B_V_DIM].astype(vb_ref.dtype)
        vb_ref[0, h, :, B_V_DIM:] = first_col


def _prep(x, w_qkv, gq, gk, tables, tm):
    bsz, seq, _ = x.shape
    nblk = seq // tm
    gsum = jnp.kron(jnp.eye(LANES // HEAD_DIM, dtype=F32), jnp.ones((HEAD_DIM, HEAD_DIM), F32)).astype(BF16)
    tok = lambda b, i: (i, 0)
    const = lambda b, i: (0, 0)
    head_out = lambda nh, w: pl.BlockSpec((1, nh, tm, w), lambda b, i: (b, 0, i, 0))
    out_shape = (
        jax.ShapeDtypeStruct((bsz, A_Q_HEADS, seq, HEAD_DIM), BF16),
        jax.ShapeDtypeStruct((bsz, A_KV_HEADS, seq, HEAD_DIM), BF16),
        jax.ShapeDtypeStruct((bsz, A_KV_HEADS, seq, LANES), BF16),
        jax.ShapeDtypeStruct((bsz, 2 * B_HEADS, seq, HEAD_DIM), BF16),
        jax.ShapeDtypeStruct((bsz, 2 * B_HEADS, seq, HEAD_DIM), BF16),
        jax.ShapeDtypeStruct((bsz, B_HEADS, seq, 2 * B_V_DIM), BF16),
    )
    return pl.pallas_call(
        _prep_kernel,
        grid=(bsz, nblk),
        in_specs=[
            pl.BlockSpec((1, tm, D_MODEL), lambda b, i: (b, i, 0)),
            pl.BlockSpec((D_MODEL, COLS_QKV), const),
            pl.BlockSpec((1, LANES), const),
            pl.BlockSpec((1, LANES), const),
        ] + [pl.BlockSpec((tm, LANES), tok)] * 6 + [pl.BlockSpec((LANES, LANES), const)],
        out_specs=(
            head_out(A_Q_HEADS, HEAD_DIM), head_out(A_KV_HEADS, HEAD_DIM), head_out(A_KV_HEADS, LANES),
            head_out(2 * B_HEADS, HEAD_DIM), head_out(2 * B_HEADS, HEAD_DIM), head_out(B_HEADS, 2 * B_V_DIM),
        ),
        out_shape=out_shape,
        compiler_params=_cparams(("parallel", "parallel")),
        name="prep",
    )(x, w_qkv, gq, gk, *tables, gsum)


def _flash_kernel(q_ref, k_ref, v_ref, o_ref, acc_ref, m_ref, *, tk, dv):
    group, tq, _ = q_ref.shape[1:]
    rows = group * tq
    q = q_ref[0].reshape(rows, HEAD_DIM)
    acc_ref[...] = jnp.zeros_like(acc_ref)
    m_ref[...] = jnp.full_like(m_ref, -jnp.inf)

    def body(j, carry):
        start = pl.multiple_of(j * tk, tk)
        k = k_ref[0, 0, pl.ds(start, tk), :]
        v = v_ref[0, 0, pl.ds(start, tk), :]
        s = lax.dot_general(q, k, (((1,), (1,)), ((), ())), preferred_element_type=F32)
        m_prev = m_ref[...]
        m_new = jnp.maximum(m_prev, jnp.max(s, axis=1, keepdims=True))
        p = jnp.exp(s - m_new)
        alpha = jnp.exp(m_prev - m_new)
        acc_ref[...] = alpha * acc_ref[...] + jnp.dot(p.astype(BF16), v, preferred_element_type=F32)
        m_ref[...] = m_new
        return carry

    lax.fori_loop(0, k_ref.shape[2] // tk, body, 0)
    acc = acc_ref[...]
    o = acc[:, :dv] / acc[:, dv:dv + 1]
    o_ref[0] = o.reshape(group, tq, dv).astype(o_ref.dtype)


def _flash(q, k, v, *, kv_of, v_of, tq, tk, dv):
    n, group, seq, _ = q.shape
    dve = v.shape[-1]
    return pl.pallas_call(
        functools.partial(_flash_kernel, tk=tk, dv=dv),
        grid=(n, seq // tq),
        in_specs=[
            pl.BlockSpec((1, group, tq, HEAD_DIM), lambda b, i: (b, 0, i, 0)),
            pl.BlockSpec((1, 1, seq, HEAD_DIM), lambda b, i: (kv_of(b), 0, 0, 0)),
            pl.BlockSpec((1, 1, seq, dve), lambda b, i: (v_of(b), 0, 0, 0)),
        ],
        out_specs=pl.BlockSpec((1, group, tq, dv), lambda b, i: (b, 0, i, 0)),
        out_shape=jax.ShapeDtypeStruct((n, group, seq, dv), BF16),
        scratch_shapes=[pltpu.VMEM((group * tq, dve), F32), pltpu.VMEM((group * tq, 1), F32)],
        compiler_params=_cparams(("parallel", "parallel")),
        name="flash_dv%d" % dv,
    )(q, k, v)


def _layer_norm(x, g, b):
    mu = jnp.mean(x, axis=-1, keepdims=True)
    xc = x - mu
    var = jnp.mean(xc * xc, axis=-1, keepdims=True)
    return xc * lax.rsqrt(var + LN_EPS) * g + b


def _merge_kernel(x_ref, oa_ref, ob_ref, wg_ref, bg_ref, wpa_ref, wpb_ref, wo_ref, sub_ref,
                  lq1_ref, lk1_ref, lq2_ref, lk2_ref, g_ref, b_ref, h_ref, ht_ref):
    x = x_ref[0]
    xb = x.astype(BF16)
    lam = (jnp.exp(jnp.sum(lq1_ref[...] * lk1_ref[...], axis=-1, keepdims=True))
           - jnp.exp(jnp.sum(lq2_ref[...] * lk2_ref[...], axis=-1, keepdims=True)) + LAMBDA_INIT)
    gl = jnp.dot(xb, wg_ref[...], preferred_element_type=F32) + bg_ref[...]
    gate_a = jax.nn.sigmoid(gl[:, :D_MODEL])
    gate_b = jax.nn.sigmoid(gl[:, D_MODEL:])

    pa = jnp.zeros((x.shape[0], D_MODEL), F32)
    for h in range(A_Q_HEADS):
        pa = pa + jnp.dot(oa_ref[0, h], wpa_ref[h * HEAD_DIM:(h + 1) * HEAD_DIM, :],
                          preferred_element_type=F32)
    pb = jnp.zeros((x.shape[0], D_MODEL), F32)
    for h in range(B_HEADS):
        y = ob_ref[0, 2 * h].astype(F32) - lam * ob_ref[0, 2 * h + 1].astype(F32)
        y = y * lax.rsqrt(jnp.mean(y * y, axis=-1, keepdims=True) + EPS) * sub_ref[...]
        y = y * (1.0 - LAMBDA_INIT)
        pb = pb + jnp.dot(y.astype(BF16), wpb_ref[h * B_V_DIM:(h + 1) * B_V_DIM, :],
                          preferred_element_type=F32)
    merged = gate_a * pa + gate_b * pb
    pre = DN_ALPHA * x + jnp.dot(merged.astype(BF16), wo_ref[...], preferred_element_type=F32)
    h_out = _layer_norm(pre, g_ref[...], b_ref[...])
    h_ref[0] = h_out
    ht_ref[...] = h_out.T.astype(ht_ref.dtype)


def _merge(x, oa, ob, wg, bg, wpa, wpb, wo, sub, lams, g, b, tm):
    bsz, seq, _ = x.shape
    nblk = seq // tm
    const = lambda bb, i: (0, 0)
    vec = lambda w: pl.BlockSpec((1, w), const)
    return pl.pallas_call(
        _merge_kernel,
        grid=(bsz, nblk),
        in_specs=[
            pl.BlockSpec((1, tm, D_MODEL), lambda bb, i: (bb, i, 0)),
            pl.BlockSpec((1, A_Q_HEADS, tm, HEAD_DIM), lambda bb, i: (bb, 0, i, 0)),
            pl.BlockSpec((1, 2 * B_HEADS, tm, B_V_DIM), lambda bb, i: (bb, 0, i, 0)),
            pl.BlockSpec((D_MODEL, 2 * D_MODEL), const),
            vec(2 * D_MODEL),
            pl.BlockSpec((COLS_A_Q, D_MODEL), const),
            pl.BlockSpec((COLS_B_V, D_MODEL), const),
            pl.BlockSpec((D_MODEL, D_MODEL), const),
            vec(B_V_DIM), vec(HEAD_DIM), vec(HEAD_DIM), vec(HEAD_DIM), vec(HEAD_DIM),
            vec(D_MODEL), vec(D_MODEL),
        ],
        out_specs=(
            pl.BlockSpec((1, tm, D_MODEL), lambda bb, i: (bb, i, 0)),
            pl.BlockSpec((D_MODEL, tm), lambda bb, i: (0, bb * nblk + i)),
        ),
        out_shape=(
            jax.ShapeDtypeStruct((bsz, seq, D_MODEL), F32),
            jax.ShapeDtypeStruct((D_MODEL, bsz * seq), BF16),
        ),
        compiler_params=_cparams(("parallel", "parallel")),
        name="merge",
    )(x, oa, ob, wg, bg, wpa, wpb, wo, sub, *lams, g, b)


def _top_values(s, n):
    vals = []
    for _ in range(n):
        m = jnp.max(s, axis=0, keepdims=True)
        vals.append(m)
        s = jnp.where(s == m, -jnp.inf, s)
    return vals


def _route_kernel(ht_ref, wq_ref, keys_ref, thr_ref, coef_ref, s2_ref, e2_ref):
    qt = jnp.dot(wq_ref[...], ht_ref[...], preferred_element_type=F32).astype(BF16)
    s1 = jnp.dot(keys_ref[0, 0], qt[:PEER_HALF], preferred_element_type=F32)
    s2 = jnp.dot(keys_ref[0, 1], qt[PEER_HALF:], preferred_element_type=F32)
    v1 = _top_values(s1, PEER_TOPK)
    v2 = _top_values(s2, PEER_TOPK)
    cand = jnp.concatenate([v1[k] + v2[l] for k, l in _CAND_PAIRS], axis=0)
    top = _top_values(cand, PEER_TOPK + 1)
    thr = 0.5 * (top[PEER_TOPK - 1] + top[PEER_TOPK])
    ev1 = [jnp.exp(v - v1[0]) for v in v1]
    ev2 = [jnp.exp(v - v2[0]) for v in v2]
    z = jnp.zeros_like(thr)
    for k, l in _CAND_PAIRS:
        z = z + jnp.where(v1[k] + v2[l] > thr, ev1[k] * ev2[l], 0.0)
    thr_ref[0] = thr - s1
    coef_ref[0] = jnp.exp(s1 - v1[0]) / z
    s2_ref[0] = s2
    e2_ref[0] = jnp.exp(s2 - v2[0])


def _route(ht, wq_t, keys, tm):
    ntok = ht.shape[1]
    out = jax.ShapeDtypeStruct((PEER_HEADS, PEER_N_KEYS, ntok), F32)
    ospec = pl.BlockSpec((1, PEER_N_KEYS, tm), lambda t, h: (h, 0, t))
    return pl.pallas_call(
        _route_kernel,
        grid=(ntok // tm, PEER_HEADS),
        in_specs=[
            pl.BlockSpec((D_MODEL, tm), lambda t, h: (0, t)),
            pl.BlockSpec((2 * PEER_HALF, D_MODEL), lambda t, h: (h, 0)),
            pl.BlockSpec((1, 2, PEER_N_KEYS, PEER_HALF), lambda t, h: (h, 0, 0, 0)),
        ],
        out_specs=(ospec, ospec, ospec, ospec),
        out_shape=(out, out, out, out),
        compiler_params=_cparams(("parallel", "parallel")),
        name="route",
    )(ht, wq_t, keys)


def _gelu(a):
    return 0.5 * a * (1.0 + lax.erf(a * (2.0 ** -0.5)))


def _peer_kernel(ht_ref, u_ref, vt_ref, thr_ref, coef_ref, s2_ref, e2_ref, h_ref, g_ref, b_ref,
                 o_ref, yt_ref, *, rows_per_step):
    e = pl.program_id(1)

    @pl.when(e == 0)
    def _():
        yt_ref[...] = jnp.zeros_like(yt_ref)

    at = jnp.dot(u_ref[...], ht_ref[...], preferred_element_type=F32)
    pieces = []
    for r in range(rows_per_step):
        gate = jnp.zeros((PEER_N_KEYS, at.shape[1]), F32)
        for h in range(PEER_HEADS):
            sel = s2_ref[h] >= thr_ref[h, r:r + 1, :]
            gate = gate + jnp.where(sel, e2_ref[h] * coef_ref[h, r:r + 1, :], 0.0)
        a = at[r * PEER_N_KEYS:(r + 1) * PEER_N_KEYS, :]
        pieces.append((_gelu(a) * gate).astype(BF16))
    wt = jnp.concatenate(pieces, axis=0)
    yt_ref[...] += jnp.dot(vt_ref[...], wt, preferred_element_type=F32)

    @pl.when(e == pl.num_programs(1) - 1)
    def _():
        pre = DN_ALPHA * h_ref[...] + yt_ref[...].T
        o_ref[...] = _layer_norm(pre, g_ref[...], b_ref[...])


def _peer(ht, u, vt, thr, coef, s2, e2, h, g, b, tm, eb):
    ntok = ht.shape[1]
    rows = eb // PEER_N_KEYS
    full = pl.BlockSpec((PEER_HEADS, PEER_N_KEYS, tm), lambda t, e: (0, 0, t))
    part = pl.BlockSpec((PEER_HEADS, rows, tm), lambda t, e: (0, e, t))
    vec = pl.BlockSpec((1, D_MODEL), lambda t, e: (0, 0))
    return pl.pallas_call(
        functools.partial(_peer_kernel, rows_per_step=rows),
        grid=(ntok // tm, PEER_N_EXPERTS // eb),
        in_specs=[
            pl.BlockSpec((D_MODEL, tm), lambda t, e: (0, t)),
            pl.BlockSpec((eb, D_MODEL), lambda t, e: (e, 0)),
            pl.BlockSpec((D_MODEL, eb), lambda t, e: (0, e)),
            part, part, full, full,
            pl.BlockSpec((tm, D_MODEL), lambda t, e: (t, 0)),
            vec, vec,
        ],
        out_specs=pl.BlockSpec((tm, D_MODEL), lambda t, e: (t, 0)),
        out_shape=jax.ShapeDtypeStruct((ntok, D_MODEL), F32),
        scratch_shapes=[pltpu.VMEM((D_MODEL, tm), F32)],
        compiler_params=_cparams(("parallel", "arbitrary")),
        name="peer",
    )(ht, u, vt, thr, coef, s2, e2, h, g, b)


def _hybrid_block(x, w_in, b_gate, q_norm_a, k_norm_a, lambda_q1, lambda_k1, lambda_q2, lambda_k2,
                  subln_b, w_proj_a, w_proj_b, w_out, ln1_g, ln1_b, w_query, sub_keys,
                  u_table, v_table, ln2_g, ln2_b, *, tm_prep, tq_a, tq_b, tk, tm_merge, tm_route,
                  tm_peer, eb):
    bsz, seq, _ = x.shape
    row = lambda a: a.reshape(1, -1).astype(F32)
    two = lambda a: jnp.concatenate([row(a), row(a)], axis=-1)

    qa, ka, va, qb, kb, vb = _prep(x, w_in[:, :COLS_QKV].astype(BF16), two(q_norm_a), two(k_norm_a),
                                   _rope_tables(seq), tm_prep)

    oa = _flash(qa.reshape(bsz * A_KV_HEADS, A_GROUP, seq, HEAD_DIM),
                ka.reshape(bsz * A_KV_HEADS, 1, seq, HEAD_DIM),
                va.reshape(bsz * A_KV_HEADS, 1, seq, LANES),
                kv_of=lambda n: n, v_of=lambda n: n, tq=tq_a, tk=tk, dv=HEAD_DIM)
    ob = _flash(qb.reshape(bsz * 2 * B_HEADS, 1, seq, HEAD_DIM),
                kb.reshape(bsz * 2 * B_HEADS, 1, seq, HEAD_DIM),
                vb.reshape(bsz * B_HEADS, 1, seq, 2 * B_V_DIM),
                kv_of=lambda n: n, v_of=lambda n: n // 2, tq=tq_b, tk=tk, dv=B_V_DIM)

    h, ht = _merge(x, oa.reshape(bsz, A_Q_HEADS, seq, HEAD_DIM), ob.reshape(bsz, 2 * B_HEADS, seq, B_V_DIM),
                   w_in[:, COLS_QKV:].astype(BF16), row(b_gate), w_proj_a.astype(BF16),
                   w_proj_b.astype(BF16), w_out.astype(BF16), row(subln_b),
                   (row(lambda_q1), row(lambda_k1), row(lambda_q2), row(lambda_k2)),
                   row(ln1_g), row(ln1_b), tm_merge)

    thr, coef, s2, e2 = _route(ht, w_query.T.astype(BF16), sub_keys.astype(BF16), tm_route)
    out = _peer(ht, u_table.astype(BF16), v_table.T.astype(BF16), thr, coef, s2, e2,
                h.reshape(bsz * seq, D_MODEL), row(ln2_g), row(ln2_b), tm_peer, eb)
    return out.reshape(bsz, seq, D_MODEL)


def kernel(x, w_in, b_gate, q_norm_a, k_norm_a, lambda_q1, lambda_k1, lambda_q2, lambda_k2, subln_b,
           w_proj_a, w_proj_b, w_out, ln1_g, ln1_b, w_query, sub_keys, u_table, v_table, ln2_g, ln2_b):
    return _hybrid_block(
        x, w_in[0], b_gate[0], q_norm_a[0], k_norm_a[0], lambda_q1[0], lambda_k1[0], lambda_q2[0],
        lambda_k2[0], subln_b[0], w_proj_a[0], w_proj_b[0], w_out[0], ln1_g[0], ln1_b[0], w_query[0],
        sub_keys[0], u_table[0], v_table[0], ln2_g[0], ln2_b[0],
        tm_prep=512, tq_a=256, tq_b=1024, tk=512, tm_merge=512, tm_route=512, tm_peer=512, eb=1024)
```

```python
import functools
import math

import jax
import jax.numpy as jnp
from jax import lax
from jax.experimental import pallas as pl
from jax.experimental.pallas import tpu as pltpu

F32 = jnp.float32
BF16 = jnp.bfloat16

D_MODEL = 1024
HEAD_DIM = 64
A_Q_HEADS = 8
A_KV_HEADS = 2
A_GROUP = A_Q_HEADS // A_KV_HEADS
B_HEADS = 4
B_V_DIM = 2 * HEAD_DIM
GRID_W = 64
ROPE_THETA = 10000.0
EPS = 1e-6
LN_EPS = 1e-5
PEER_HEADS = 8
PEER_N_KEYS = 128
PEER_N_EXPERTS = PEER_N_KEYS * PEER_N_KEYS
PEER_HALF = 128
PEER_TOPK = 16
DEPTH = 1
DN_ALPHA = (2.0 * DEPTH) ** 0.25
LAMBDA_INIT = 0.8 - 0.6 * math.exp(-0.3 * 0)

COLS_A_Q = A_Q_HEADS * HEAD_DIM
COLS_A_KV = A_KV_HEADS * HEAD_DIM
COLS_B_QK = B_HEADS * 2 * HEAD_DIM
COLS_B_V = B_HEADS * B_V_DIM
COLS_QKV = COLS_A_Q + 2 * COLS_A_KV + 2 * COLS_B_QK + COLS_B_V

LANES = 128
BF16_ROWS = 16
VMEM_LIMIT = 48 * 1024 * 1024
LOG2E = math.log2(math.e)

_CAND_PAIRS = tuple((k, l) for k in range(PEER_TOPK + 1) for l in range(PEER_TOPK + 1)
                    if (k + 1) * (l + 1) <= PEER_TOPK + 1)


def _cparams(sem):
    return pltpu.CompilerParams(dimension_semantics=sem, vmem_limit_bytes=VMEM_LIMIT)


def _rope_tables(seq):
    t = jnp.arange(seq, dtype=jnp.int32)

    def tab(pos, dim):
        freqs = ROPE_THETA ** (-jnp.arange(0, dim, 2, dtype=F32) / dim)
        ang = pos.astype(F32)[:, None] * freqs[None, :]
        return jnp.cos(ang), jnp.sin(ang)

    cr, sr = tab(t // GRID_W, HEAD_DIM // 2)
    cc, sc = tab(t % GRID_W, HEAD_DIM // 2)
    ct, st = tab(t, HEAD_DIM)
    z16 = jnp.zeros_like(sr)
    z32 = jnp.zeros_like(st)
    cos_a = jnp.concatenate([cr, cr, cc, cc], axis=-1)
    up_a = jnp.concatenate([-sr, z16, -sc, z16], axis=-1)
    dn_a = jnp.concatenate([z16, sr, z16, sc], axis=-1)
    cos_b = jnp.concatenate([ct, ct], axis=-1)
    up_b = jnp.concatenate([-st, z32], axis=-1)
    dn_b = jnp.concatenate([z32, st], axis=-1)
    two = lambda a: jnp.concatenate([a, a], axis=-1)
    return tuple(two(a) for a in (cos_a, up_a, dn_a, cos_b, up_b, dn_b))


def _rotate(x, cos, up, dn, half):
    n = x.shape[-1]
    return x * cos + pltpu.roll(x, n - half, 1) * up + pltpu.roll(x, half, 1) * dn


def _prep_kernel(x_ref, w_ref, gq_ref, gk_ref, ca_ref, ua_ref, da_ref, cb_ref, ub_ref, db_ref,
                 gsum_ref, qa_ref, ka_ref, va_ref, qb_ref, kb_ref, vb_ref):
    xb = x_ref[0].astype(BF16)
    tm = xb.shape[0]
    q_scale = HEAD_DIM ** -0.5 * LOG2E
    ones_rows = (lax.broadcasted_iota(jnp.int32, (BF16_ROWS, tm), 0) == 0).astype(BF16)

    def proj(c0, width):
        return jnp.dot(xb, w_ref[:, c0:c0 + width], preferred_element_type=F32)

    def norm_rope_a(y, gain):
        ss = jnp.dot((y * y).astype(BF16), gsum_ref[...], preferred_element_type=F32)
        y = y * lax.rsqrt(ss * (1.0 / HEAD_DIM) + EPS) * gain
        return _rotate(y, ca_ref[...], ua_ref[...], da_ref[...], HEAD_DIM // 4)

    def rope_b(y):
        return _rotate(y, cb_ref[...], ub_ref[...], db_ref[...], HEAD_DIM // 2)

    def store_q(ref, y, c):
        yt = (y * q_scale).T.astype(ref.dtype)
        ref[0, 2 * c] = yt[:HEAD_DIM]
        ref[0, 2 * c + 1] = yt[HEAD_DIM:]

    def store_k(ref, y, c):
        y = y.astype(ref.dtype)
        ref[0, 2 * c] = y[:, :HEAD_DIM]
        ref[0, 2 * c + 1] = y[:, HEAD_DIM:]

    c0 = 0
    qa = proj(c0, COLS_A_Q)
    for c in range(COLS_A_Q // LANES):
        store_q(qa_ref, norm_rope_a(qa[:, c * LANES:(c + 1) * LANES], gq_ref[...]), c)
    c0 += COLS_A_Q
    store_k(ka_ref, norm_rope_a(proj(c0, COLS_A_KV), gk_ref[...]), 0)
    c0 += COLS_A_KV
    vat = proj(c0, COLS_A_KV).T.astype(va_ref.dtype)
    for h in range(A_KV_HEADS):
        va_ref[0, h, 0, :HEAD_DIM] = vat[h * HEAD_DIM:(h + 1) * HEAD_DIM]
        va_ref[0, h, 0, HEAD_DIM:] = ones_rows
    c0 += COLS_A_KV
    qb = proj(c0, COLS_B_QK)
    for c in range(COLS_B_QK // LANES):
        store_q(qb_ref, rope_b(qb[:, c * LANES:(c + 1) * LANES]), c)
    c0 += COLS_B_QK
    kb = proj(c0, COLS_B_QK)
    for c in range(COLS_B_QK // LANES):
        store_k(kb_ref, rope_b(kb[:, c * LANES:(c + 1) * LANES]), c)
    c0 += COLS_B_QK
    vb = proj(c0, COLS_B_V)
    for h in range(B_HEADS):
        vb_ref[0, h, 0, :B_V_DIM] = vb[:, h * B_V_DIM:(h + 1) * B_V_DIM].T.astype(vb_ref.dtype)
        vb_ref[0, h, 0, B_V_DIM:] = ones_rows


def _prep(x, w_qkv, gq, gk, tables, tm):
    bsz, seq, _ = x.shape
    nblk = seq // tm
    gsum = jnp.kron(jnp.eye(LANES // HEAD_DIM, dtype=F32), jnp.ones((HEAD_DIM, HEAD_DIM), F32)).astype(BF16)
    tok = lambda b, i: (i, 0)
    const = lambda b, i: (0, 0)
    qt_out = lambda nh: pl.BlockSpec((1, nh, HEAD_DIM, tm), lambda b, i: (b, 0, 0, i))
    k_out = lambda nh: pl.BlockSpec((1, nh, tm, HEAD_DIM), lambda b, i: (b, 0, i, 0))
    vt_out = lambda nh, r: pl.BlockSpec((1, nh, 1, r, tm), lambda b, i: (b, 0, i, 0, 0))
    rows_a = HEAD_DIM + BF16_ROWS
    rows_b = B_V_DIM + BF16_ROWS
    out_shape = (
        jax.ShapeDtypeStruct((bsz, A_Q_HEADS, HEAD_DIM, seq), BF16),
        jax.ShapeDtypeStruct((bsz, A_KV_HEADS, seq, HEAD_DIM), BF16),
        jax.ShapeDtypeStruct((bsz, A_KV_HEADS, nblk, rows_a, tm), BF16),
        jax.ShapeDtypeStruct((bsz, 2 * B_HEADS, HEAD_DIM, seq), BF16),
        jax.ShapeDtypeStruct((bsz, 2 * B_HEADS, seq, HEAD_DIM), BF16),
        jax.ShapeDtypeStruct((bsz, B_HEADS, nblk, rows_b, tm), BF16),
    )
    return pl.pallas_call(
        _prep_kernel,
        grid=(bsz, nblk),
        in_specs=[
            pl.BlockSpec((1, tm, D_MODEL), lambda b, i: (b, i, 0)),
            pl.BlockSpec((D_MODEL, COLS_QKV), const),
            pl.BlockSpec((1, LANES), const),
            pl.BlockSpec((1, LANES), const),
        ] + [pl.BlockSpec((tm, LANES), tok)] * 6 + [pl.BlockSpec((LANES, LANES), const)],
        out_specs=(qt_out(A_Q_HEADS), k_out(A_KV_HEADS), vt_out(A_KV_HEADS, rows_a),
                   qt_out(2 * B_HEADS), k_out(2 * B_HEADS), vt_out(B_HEADS, rows_b)),
        out_shape=out_shape,
        compiler_params=_cparams(("parallel", "parallel")),
        name="prep",
    )(x, w_qkv, gq, gk, *tables, gsum)


def _flash_kernel(qt_ref, k_ref, vt_ref, o_ref, qcat_ref, acc_ref, m_ref, s0_ref, s1_ref, *, dv):
    group, _, tq = qt_ref.shape[1:]
    nchunk, _, tk = vt_ref.shape[2:]
    for r in range(group):
        qcat_ref[:, r * tq:(r + 1) * tq] = qt_ref[0, r]
    acc_ref[...] = jnp.zeros_like(acc_ref)
    m_ref[...] = jnp.full_like(m_ref, -jnp.inf)

    def scores(j):
        start = pl.multiple_of(j * tk, tk)
        k = k_ref[0, 0, pl.ds(start, tk), :]
        return jnp.dot(k, qcat_ref[...], preferred_element_type=F32)

    def update(s_ref, j):
        st = s_ref[...]
        m_prev = m_ref[...]
        m_new = jnp.maximum(m_prev, jnp.max(st, axis=0, keepdims=True))
        pt = jnp.exp2(st - m_new).astype(BF16)
        alpha = jnp.exp2(m_prev - m_new)
        acc_ref[...] = alpha * acc_ref[...] + jnp.dot(vt_ref[0, 0, j], pt, preferred_element_type=F32)
        m_ref[...] = m_new

    s0_ref[...] = scores(0)

    def body(i, carry):
        j = 2 * i
        s1_ref[...] = scores(j + 1)
        update(s0_ref, j)
        s0_ref[...] = scores(j + 2)
        update(s1_ref, j + 1)
        return carry

    lax.fori_loop(0, nchunk // 2 - 1, body, 0)
    s1_ref[...] = scores(nchunk - 1)
    update(s0_ref, nchunk - 2)
    update(s1_ref, nchunk - 1)
    acc = acc_ref[...]
    ot = (acc[:dv] / acc[dv:dv + 1]).astype(o_ref.dtype)
    for r in range(group):
        o_ref[0, r] = ot[:, r * tq:(r + 1) * tq]


def _flash(qt, k, vt, *, v_of, tq, dv):
    n, group, _, seq = qt.shape
    nchunk, rows, tk = vt.shape[2:]
    return pl.pallas_call(
        functools.partial(_flash_kernel, dv=dv),
        grid=(n, seq // tq),
        in_specs=[
            pl.BlockSpec((1, group, HEAD_DIM, tq), lambda b, i: (b, 0, 0, i)),
            pl.BlockSpec((1, 1, seq, HEAD_DIM), lambda b, i: (b, 0, 0, 0)),
            pl.BlockSpec((1, 1, nchunk, rows, tk), lambda b, i: (v_of(b), 0, 0, 0, 0)),
        ],
        out_specs=pl.BlockSpec((1, group, dv, tq), lambda b, i: (b, 0, 0, i)),
        out_shape=jax.ShapeDtypeStruct((n, group, dv, seq), BF16),
        scratch_shapes=[pltpu.VMEM((HEAD_DIM, group * tq), BF16),
                        pltpu.VMEM((rows, group * tq), F32),
                        pltpu.VMEM((1, group * tq), F32),
                        pltpu.VMEM((tk, group * tq), F32),
                        pltpu.VMEM((tk, group * tq), F32)],
        compiler_params=_cparams(("parallel", "parallel")),
        name="flash_dv%d" % dv,
    )(qt, k, vt)


def _layer_norm(x, g, b):
    mu = jnp.mean(x, axis=-1, keepdims=True)
    xc = x - mu
    var = jnp.mean(xc * xc, axis=-1, keepdims=True)
    return xc * lax.rsqrt(var + LN_EPS) * g + b


def _dot_t(at, b):
    return lax.dot_general(at, b, (((0,), (0,)), ((), ())), preferred_element_type=F32)


def _merge_kernel(x_ref, oa_ref, ob_ref, wg_ref, bg_ref, wpa_ref, wpb_ref, wo_ref, sub_ref,
                  lq1_ref, lk1_ref, lq2_ref, lk2_ref, g_ref, b_ref, h_ref, ht_ref):
    x = x_ref[0]
    xb = x.astype(BF16)
    tm = x.shape[0]
    lam = (jnp.exp(jnp.sum(lq1_ref[...] * lk1_ref[...], axis=-1, keepdims=True))
           - jnp.exp(jnp.sum(lq2_ref[...] * lk2_ref[...], axis=-1, keepdims=True)) + LAMBDA_INIT)
    gl = jnp.dot(xb, wg_ref[...], preferred_element_type=F32) + bg_ref[...]
    gate_a = jax.nn.sigmoid(gl[:, :D_MODEL])
    gate_b = jax.nn.sigmoid(gl[:, D_MODEL:])

    pa = _dot_t(oa_ref[0].reshape(COLS_A_Q, tm), wpa_ref[...])
    ybt = []
    for h in range(B_HEADS):
        y = ob_ref[0, 2 * h].astype(F32) - lam * ob_ref[0, 2 * h + 1].astype(F32)
        y = y * lax.rsqrt(jnp.mean(y * y, axis=0, keepdims=True) + EPS) * sub_ref[...]
        ybt.append((y * (1.0 - LAMBDA_INIT)).astype(BF16))
    pb = _dot_t(jnp.concatenate(ybt, axis=0), wpb_ref[...])
    merged = gate_a * pa + gate_b * pb
    pre = DN_ALPHA * x + jnp.dot(merged.astype(BF16), wo_ref[...], preferred_element_type=F32)
    h_out = _layer_norm(pre, g_ref[...], b_ref[...])
    h_ref[0] = h_out
    ht_ref[...] = h_out.T.astype(ht_ref.dtype)


def _merge(x, oat, obt, wg, bg, wpa, wpb, wo, sub, lams, g, b, tm):
    bsz, seq, _ = x.shape
    nblk = seq // tm
    const = lambda bb, i: (0, 0)
    vec = lambda w: pl.BlockSpec((1, w), const)
    return pl.pallas_call(
        _merge_kernel,
        grid=(bsz, nblk),
        in_specs=[
            pl.BlockSpec((1, tm, D_MODEL), lambda bb, i: (bb, i, 0)),
            pl.BlockSpec((1, A_Q_HEADS, HEAD_DIM, tm), lambda bb, i: (bb, 0, 0, i)),
            pl.BlockSpec((1, 2 * B_HEADS, B_V_DIM, tm), lambda bb, i: (bb, 0, 0, i)),
            pl.BlockSpec((D_MODEL, 2 * D_MODEL), const),
            vec(2 * D_MODEL),
            pl.BlockSpec((COLS_A_Q, D_MODEL), const),
            pl.BlockSpec((COLS_B_V, D_MODEL), const),
            pl.BlockSpec((D_MODEL, D_MODEL), const),
            pl.BlockSpec((B_V_DIM, 1), const),
            vec(HEAD_DIM), vec(HEAD_DIM), vec(HEAD_DIM), vec(HEAD_DIM),
            vec(D_MODEL), vec(D_MODEL),
        ],
        out_specs=(
            pl.BlockSpec((1, tm, D_MODEL), lambda bb, i: (bb, i, 0)),
            pl.BlockSpec((D_MODEL, tm), lambda bb, i: (0, bb * nblk + i)),
        ),
        out_shape=(
            jax.ShapeDtypeStruct((bsz, seq, D_MODEL), F32),
            jax.ShapeDtypeStruct((D_MODEL, bsz * seq), BF16),
        ),
        compiler_params=_cparams(("parallel", "parallel")),
        name="merge",
    )(x, oat, obt, wg, bg, wpa, wpb, wo, sub, *lams, g, b)


def _top_values(s, n):
    vals = []
    for _ in range(n):
        m = jnp.max(s, axis=0, keepdims=True)
        vals.append(m)
        s = jnp.where(s == m, -jnp.inf, s)
    return vals


def _route_kernel(ht_ref, wq_ref, keys_ref, thr_ref, coef_ref, s2_ref, e2_ref):
    qt = jnp.dot(wq_ref[...], ht_ref[...], preferred_element_type=F32).astype(BF16)
    s1 = jnp.dot(keys_ref[0, 0], qt[:PEER_HALF], preferred_element_type=F32)
    s2 = jnp.dot(keys_ref[0, 1], qt[PEER_HALF:], preferred_element_type=F32)
    v1 = _top_values(s1, PEER_TOPK + 1)
    v2 = _top_values(s2, PEER_TOPK + 1)
    cand = jnp.concatenate([v1[k] + v2[l] for k, l in _CAND_PAIRS], axis=0)
    top = _top_values(cand, PEER_TOPK + 1)
    thr = 0.5 * (top[PEER_TOPK - 1] + top[PEER_TOPK])
    ev1 = [jnp.exp(v - v1[0]) for v in v1]
    ev2 = [jnp.exp(v - v2[0]) for v in v2]
    z = jnp.zeros_like(thr)
    for k, l in _CAND_PAIRS:
        z = z + jnp.where(v1[k] + v2[l] > thr, ev1[k] * ev2[l], 0.0)
    thr_ref[0] = thr - s1
    coef_ref[0] = jnp.exp(s1 - v1[0]) / z
    s2_ref[0] = s2
    e2_ref[0] = jnp.exp(s2 - v2[0])


def _route(ht, wq_t, keys, tm):
    ntok = ht.shape[1]
    out = jax.ShapeDtypeStruct((PEER_HEADS, PEER_N_KEYS, ntok), F32)
    ospec = pl.BlockSpec((1, PEER_N_KEYS, tm), lambda t, h: (h, 0, t))
    return pl.pallas_call(
        _route_kernel,
        grid=(ntok // tm, PEER_HEADS),
        in_specs=[
            pl.BlockSpec((D_MODEL, tm), lambda t, h: (0, t)),
            pl.BlockSpec((2 * PEER_HALF, D_MODEL), lambda t, h: (h, 0)),
            pl.BlockSpec((1, 2, PEER_N_KEYS, PEER_HALF), lambda t, h: (h, 0, 0, 0)),
        ],
        out_specs=(ospec, ospec, ospec, ospec),
        out_shape=(out, out, out, out),
        compiler_params=_cparams(("parallel", "parallel")),
        name="route",
    )(ht, wq_t, keys)


def _gelu(a):
    return 0.5 * a * (1.0 + lax.erf(a * (2.0 ** -0.5)))


def _peer_kernel(ht_ref, u_ref, vt_ref, thr_ref, coef_ref, s2_ref, e2_ref, h_ref, g_ref, b_ref,
                 o_ref, yt_ref, *, rows_per_step):
    e = pl.program_id(1)

    @pl.when(e == 0)
    def _():
        yt_ref[...] = jnp.zeros_like(yt_ref)

    at = jnp.dot(u_ref[...], ht_ref[...], preferred_element_type=F32)
    pieces = []
    for r in range(rows_per_step):
        gate = jnp.zeros((PEER_N_KEYS, at.shape[1]), F32)
        for h in range(PEER_HEADS):
            sel = s2_ref[h] >= thr_ref[h, r:r + 1, :]
            gate = gate + jnp.where(sel, e2_ref[h] * coef_ref[h, r:r + 1, :], 0.0)
        a = at[r * PEER_N_KEYS:(r + 1) * PEER_N_KEYS, :]
        pieces.append((_gelu(a) * gate).astype(BF16))
    wt = jnp.concatenate(pieces, axis=0)
    yt_ref[...] += jnp.dot(vt_ref[...], wt, preferred_element_type=F32)

    @pl.when(e == pl.num_programs(1) - 1)
    def _():
        pre = DN_ALPHA * h_ref[...] + yt_ref[...].T
        o_ref[...] = _layer_norm(pre, g_ref[...], b_ref[...])


def _peer(ht, u, vt, thr, coef, s2, e2, h, g, b, tm, eb):
    ntok = ht.shape[1]
    rows = eb // PEER_N_KEYS
    full = pl.BlockSpec((PEER_HEADS, PEER_N_KEYS, tm), lambda t, e: (0, 0, t))
    part = pl.BlockSpec((PEER_HEADS, rows, tm), lambda t, e: (0, e, t))
    vec = pl.BlockSpec((1, D_MODEL), lambda t, e: (0, 0))
    return pl.pallas_call(
        functools.partial(_peer_kernel, rows_per_step=rows),
        grid=(ntok // tm, PEER_N_EXPERTS // eb),
        in_specs=[
            pl.BlockSpec((D_MODEL, tm), lambda t, e: (0, t)),
            pl.BlockSpec((eb, D_MODEL), lambda t, e: (e, 0)),
            pl.BlockSpec((D_MODEL, eb), lambda t, e: (0, e)),
            part, part, full, full,
            pl.BlockSpec((tm, D_MODEL), lambda t, e: (t, 0)),
            vec, vec,
        ],
        out_specs=pl.BlockSpec((tm, D_MODEL), lambda t, e: (t, 0)),
        out_shape=jax.ShapeDtypeStruct((ntok, D_MODEL), F32),
        scratch_shapes=[pltpu.VMEM((D_MODEL, tm), F32)],
        compiler_params=_cparams(("parallel", "arbitrary")),
        name="peer",
    )(ht, u, vt, thr, coef, s2, e2, h, g, b)


def _hybrid_block(x, w_in, b_gate, q_norm_a, k_norm_a, lambda_q1, lambda_k1, lambda_q2, lambda_k2,
                  subln_b, w_proj_a, w_proj_b, w_out, ln1_g, ln1_b, w_query, sub_keys,
                  u_table, v_table, ln2_g, ln2_b, *, tk, tq_a, tq_b, tm_merge, tm_route, tm_peer, eb):
    bsz, seq, _ = x.shape
    row = lambda a: a.reshape(1, -1).astype(F32)
    two = lambda a: jnp.concatenate([row(a), row(a)], axis=-1)

    qat, ka, vat, qbt, kb, vbt = _prep(x, w_in[:, :COLS_QKV].astype(BF16), two(q_norm_a), two(k_norm_a),
                                       _rope_tables(seq), tk)

    nchunk = seq // tk
    oat = _flash(qat.reshape(bsz * A_KV_HEADS, A_GROUP, HEAD_DIM, seq),
                 ka.reshape(bsz * A_KV_HEADS, 1, seq, HEAD_DIM),
                 vat.reshape(bsz * A_KV_HEADS, 1, nchunk, HEAD_DIM + BF16_ROWS, tk),
                 v_of=lambda n: n, tq=tq_a, dv=HEAD_DIM)
    obt = _flash(qbt.reshape(bsz * 2 * B_HEADS, 1, HEAD_DIM, seq),
                 kb.reshape(bsz * 2 * B_HEADS, 1, seq, HEAD_DIM),
                 vbt.reshape(bsz * B_HEADS, 1, nchunk, B_V_DIM + BF16_ROWS, tk),
                 v_of=lambda n: n // 2, tq=tq_b, dv=B_V_DIM)

    h, ht = _merge(x, oat.reshape(bsz, A_Q_HEADS, HEAD_DIM, seq), obt.reshape(bsz, 2 * B_HEADS, B_V_DIM, seq),
                   w_in[:, COLS_QKV:].astype(BF16), row(b_gate), w_proj_a.astype(BF16),
                   w_proj_b.astype(BF16), w_out.astype(BF16), subln_b.reshape(-1, 1).astype(F32),
                   (row(lambda_q1), row(lambda_k1), row(lambda_q2), row(lambda_k2)),
                   row(ln1_g), row(ln1_b), tm_merge)

    thr, coef, s2, e2 = _route(ht, w_query.T.astype(BF16), sub_keys.astype(BF16), tm_route)
    out = _peer(ht, u_table.astype(BF16), v_table.T.astype(BF16), thr, coef, s2, e2,
                h.reshape(bsz * seq, D_MODEL), row(ln2_g), row(ln2_b), tm_peer, eb)
    return out.reshape(bsz, seq, D_MODEL)


def kernel(x, w_in, b_gate, q_norm_a, k_norm_a, lambda_q1, lambda_k1, lambda_q2, lambda_k2, subln_b,
           w_proj_a, w_proj_b, w_out, ln1_g, ln1_b, w_query, sub_keys, u_table, v_table, ln2_g, ln2_b):
    return _hybrid_block(
        x, w_in[0], b_gate[0], q_norm_a[0], k_norm_a[0], lambda_q1[0], lambda_k1[0], lambda_q2[0],
        lambda_k2[0], subln_b[0], w_proj_a[0], w_proj_b[0], w_out[0], ln1_g[0], ln1_b[0], w_query[0],
        sub_keys[0], u_table[0], v_table[0], ln2_g[0], ln2_b[0],
        tk=512, tq_a=256, tq_b=1024, tm_merge=512, tm_route=512, tm_peer=512, eb=1024)
```

```python
import functools
import math

import jax
import jax.numpy as jnp
from jax import lax
from jax.experimental import pallas as pl
from jax.experimental.pallas import tpu as pltpu

F32 = jnp.float32
BF16 = jnp.bfloat16

D_MODEL = 1024
HEAD_DIM = 64
A_Q_HEADS = 8
A_KV_HEADS = 2
A_GROUP = A_Q_HEADS // A_KV_HEADS
B_HEADS = 4
B_V_DIM = 2 * HEAD_DIM
GRID_W = 64
ROPE_THETA = 10000.0
EPS = 1e-6
LN_EPS = 1e-5
PEER_HEADS = 8
PEER_N_KEYS = 128
PEER_N_EXPERTS = PEER_N_KEYS * PEER_N_KEYS
PEER_HALF = 128
PEER_TOPK = 16
DEPTH = 1
DN_ALPHA = (2.0 * DEPTH) ** 0.25
LAMBDA_INIT = 0.8 - 0.6 * math.exp(-0.3 * 0)

COLS_A_Q = A_Q_HEADS * HEAD_DIM
COLS_A_KV = A_KV_HEADS * HEAD_DIM
COLS_B_QK = B_HEADS * 2 * HEAD_DIM
COLS_B_V = B_HEADS * B_V_DIM
COLS_QKV = COLS_A_Q + 2 * COLS_A_KV + 2 * COLS_B_QK + COLS_B_V

LANES = 128
BF16_ROWS = 16
VMEM_LIMIT = 48 * 1024 * 1024
LOG2E = math.log2(math.e)

_CAND_PAIRS = tuple((k, l) for k in range(PEER_TOPK + 1) for l in range(PEER_TOPK + 1)
                    if (k + 1) * (l + 1) <= PEER_TOPK + 1)


def _cparams(sem):
    return pltpu.CompilerParams(dimension_semantics=sem, vmem_limit_bytes=VMEM_LIMIT)


def _rope_tables(seq):
    t = jnp.arange(seq, dtype=jnp.int32)

    def tab(pos, dim):
        freqs = ROPE_THETA ** (-jnp.arange(0, dim, 2, dtype=F32) / dim)
        ang = pos.astype(F32)[:, None] * freqs[None, :]
        return jnp.cos(ang), jnp.sin(ang)

    cr, sr = tab(t // GRID_W, HEAD_DIM // 2)
    cc, sc = tab(t % GRID_W, HEAD_DIM // 2)
    ct, st = tab(t, HEAD_DIM)
    z16 = jnp.zeros_like(sr)
    z32 = jnp.zeros_like(st)
    cos_a = jnp.concatenate([cr, cr, cc, cc], axis=-1)
    up_a = jnp.concatenate([-sr, z16, -sc, z16], axis=-1)
    dn_a = jnp.concatenate([z16, sr, z16, sc], axis=-1)
    cos_b = jnp.concatenate([ct, ct], axis=-1)
    up_b = jnp.concatenate([-st, z32], axis=-1)
    dn_b = jnp.concatenate([z32, st], axis=-1)
    two = lambda a: jnp.concatenate([a, a], axis=-1)
    return tuple(two(a) for a in (cos_a, up_a, dn_a, cos_b, up_b, dn_b))


def _rotate(x, cos, up, dn, half):
    n = x.shape[-1]
    return x * cos + pltpu.roll(x, n - half, 1) * up + pltpu.roll(x, half, 1) * dn


def _prep_kernel(x_ref, w_ref, gq_ref, gk_ref, ca_ref, ua_ref, da_ref, cb_ref, ub_ref, db_ref,
                 gsum_ref, qa_ref, ka_ref, va_ref, qb_ref, kb_ref, vb_ref):
    xb = x_ref[0].astype(BF16)
    tm = xb.shape[0]
    q_scale = HEAD_DIM ** -0.5 * LOG2E
    ones_rows = (lax.broadcasted_iota(jnp.int32, (BF16_ROWS, tm), 0) == 0).astype(BF16)

    def proj(c0, width):
        return jnp.dot(xb, w_ref[:, c0:c0 + width], preferred_element_type=F32)

    def norm_rope_a(y, gain):
        ss = jnp.dot((y * y).astype(BF16), gsum_ref[...], preferred_element_type=F32)
        y = y * lax.rsqrt(ss * (1.0 / HEAD_DIM) + EPS) * gain
        return _rotate(y, ca_ref[...], ua_ref[...], da_ref[...], HEAD_DIM // 4)

    def rope_b(y):
        return _rotate(y, cb_ref[...], ub_ref[...], db_ref[...], HEAD_DIM // 2)

    def store_q(ref, y, c):
        yt = (y * q_scale).T.astype(ref.dtype)
        ref[0, 2 * c] = yt[:HEAD_DIM]
        ref[0, 2 * c + 1] = yt[HEAD_DIM:]

    def store_k(ref, y, c):
        y = y.astype(ref.dtype)
        ref[0, 2 * c] = y[:, :HEAD_DIM]
        ref[0, 2 * c + 1] = y[:, HEAD_DIM:]

    c0 = 0
    qa = proj(c0, COLS_A_Q)
    for c in range(COLS_A_Q // LANES):
        store_q(qa_ref, norm_rope_a(qa[:, c * LANES:(c + 1) * LANES], gq_ref[...]), c)
    c0 += COLS_A_Q
    store_k(ka_ref, norm_rope_a(proj(c0, COLS_A_KV), gk_ref[...]), 0)
    c0 += COLS_A_KV
    vat = proj(c0, COLS_A_KV).T.astype(va_ref.dtype)
    for h in range(A_KV_HEADS):
        va_ref[0, h, 0, :HEAD_DIM] = vat[h * HEAD_DIM:(h + 1) * HEAD_DIM]
        va_ref[0, h, 0, HEAD_DIM:] = ones_rows
    c0 += COLS_A_KV
    qb = proj(c0, COLS_B_QK)
    for c in range(COLS_B_QK // LANES):
        store_q(qb_ref, rope_b(qb[:, c * LANES:(c + 1) * LANES]), c)
    c0 += COLS_B_QK
    kb = proj(c0, COLS_B_QK)
    for c in range(COLS_B_QK // LANES):
        store_k(kb_ref, rope_b(kb[:, c * LANES:(c + 1) * LANES]), c)
    c0 += COLS_B_QK
    vb = proj(c0, COLS_B_V)
    for h in range(B_HEADS):
        vb_ref[0, h, 0, :B_V_DIM] = vb[:, h * B_V_DIM:(h + 1) * B_V_DIM].T.astype(vb_ref.dtype)
        vb_ref[0, h, 0, B_V_DIM:] = ones_rows


def _prep(x, w_qkv, gq, gk, tables, tm):
    bsz, seq, _ = x.shape
    nblk = seq // tm
    gsum = jnp.kron(jnp.eye(LANES // HEAD_DIM, dtype=F32), jnp.ones((HEAD_DIM, HEAD_DIM), F32)).astype(BF16)
    tok = lambda b, i: (i, 0)
    const = lambda b, i: (0, 0)
    qt_out = lambda nh: pl.BlockSpec((1, nh, HEAD_DIM, tm), lambda b, i: (b, 0, 0, i))
    k_out = lambda nh: pl.BlockSpec((1, nh, tm, HEAD_DIM), lambda b, i: (b, 0, i, 0))
    vt_out = lambda nh, r: pl.BlockSpec((1, nh, 1, r, tm), lambda b, i: (b, 0, i, 0, 0))
    rows_a = HEAD_DIM + BF16_ROWS
    rows_b = B_V_DIM + BF16_ROWS
    out_shape = (
        jax.ShapeDtypeStruct((bsz, A_Q_HEADS, HEAD_DIM, seq), BF16),
        jax.ShapeDtypeStruct((bsz, A_KV_HEADS, seq, HEAD_DIM), BF16),
        jax.ShapeDtypeStruct((bsz, A_KV_HEADS, nblk, rows_a, tm), BF16),
        jax.ShapeDtypeStruct((bsz, 2 * B_HEADS, HEAD_DIM, seq), BF16),
        jax.ShapeDtypeStruct((bsz, 2 * B_HEADS, seq, HEAD_DIM), BF16),
        jax.ShapeDtypeStruct((bsz, B_HEADS, nblk, rows_b, tm), BF16),
    )
    return pl.pallas_call(
        _prep_kernel,
        grid=(bsz, nblk),
        in_specs=[
            pl.BlockSpec((1, tm, D_MODEL), lambda b, i: (b, i, 0)),
            pl.BlockSpec((D_MODEL, COLS_QKV), const),
            pl.BlockSpec((1, LANES), const),
            pl.BlockSpec((1, LANES), const),
        ] + [pl.BlockSpec((tm, LANES), tok)] * 6 + [pl.BlockSpec((LANES, LANES), const)],
        out_specs=(qt_out(A_Q_HEADS), k_out(A_KV_HEADS), vt_out(A_KV_HEADS, rows_a),
                   qt_out(2 * B_HEADS), k_out(2 * B_HEADS), vt_out(B_HEADS, rows_b)),
        out_shape=out_shape,
        compiler_params=_cparams(("parallel", "parallel")),
        name="prep",
    )(x, w_qkv, gq, gk, *tables, gsum)


def _flash_kernel(qt_ref, k_ref, vt_ref, o_ref, qcat_ref, acc_ref, m_ref, s0_ref, s1_ref, *, dv):
    group, _, tq = qt_ref.shape[1:]
    nchunk, _, tk = vt_ref.shape[2:]
    for r in range(group):
        qcat_ref[:, r * tq:(r + 1) * tq] = qt_ref[0, r]
    acc_ref[...] = jnp.zeros_like(acc_ref)
    m_ref[...] = jnp.full_like(m_ref, -jnp.inf)

    def scores(j):
        start = pl.multiple_of(j * tk, tk)
        k = k_ref[0, 0, pl.ds(start, tk), :]
        return jnp.dot(k, qcat_ref[...], preferred_element_type=F32)

    def update(s_ref, j):
        st = s_ref[...]
        m_prev = m_ref[...]
        m_new = jnp.maximum(m_prev, jnp.max(st, axis=0, keepdims=True))
        pt = jnp.exp2(st - m_new).astype(BF16)
        alpha = jnp.exp2(m_prev - m_new)
        acc_ref[...] = alpha * acc_ref[...] + jnp.dot(vt_ref[0, 0, j], pt, preferred_element_type=F32)
        m_ref[...] = m_new

    s0_ref[...] = scores(0)

    def body(i, carry):
        j = 2 * i
        s1_ref[...] = scores(j + 1)
        update(s0_ref, j)
        s0_ref[...] = scores(j + 2)
        update(s1_ref, j + 1)
        return carry

    lax.fori_loop(0, nchunk // 2 - 1, body, 0)
    s1_ref[...] = scores(nchunk - 1)
    update(s0_ref, nchunk - 2)
    update(s1_ref, nchunk - 1)
    acc = acc_ref[...]
    ot = (acc[:dv] / acc[dv:dv + 1]).astype(o_ref.dtype)
    for r in range(group):
        o_ref[0, r] = ot[:, r * tq:(r + 1) * tq]


def _flash(qt, k, vt, *, v_of, tq, dv):
    n, group, _, seq = qt.shape
    nchunk, rows, tk = vt.shape[2:]
    return pl.pallas_call(
        functools.partial(_flash_kernel, dv=dv),
        grid=(n, seq // tq),
        in_specs=[
            pl.BlockSpec((1, group, HEAD_DIM, tq), lambda b, i: (b, 0, 0, i)),
            pl.BlockSpec((1, 1, seq, HEAD_DIM), lambda b, i: (b, 0, 0, 0)),
            pl.BlockSpec((1, 1, nchunk, rows, tk), lambda b, i: (v_of(b), 0, 0, 0, 0)),
        ],
        out_specs=pl.BlockSpec((1, group, dv, tq), lambda b, i: (b, 0, 0, i)),
        out_shape=jax.ShapeDtypeStruct((n, group, dv, seq), BF16),
        scratch_shapes=[pltpu.VMEM((HEAD_DIM, group * tq), BF16),
                        pltpu.VMEM((rows, group * tq), F32),
                        pltpu.VMEM((1, group * tq), F32),
                        pltpu.VMEM((tk, group * tq), F32),
                        pltpu.VMEM((tk, group * tq), F32)],
        compiler_params=_cparams(("parallel", "parallel")),
        name="flash_dv%d" % dv,
    )(qt, k, vt)


def _layer_norm(x, g, b):
    mu = jnp.mean(x, axis=-1, keepdims=True)
    xc = x - mu
    var = jnp.mean(xc * xc, axis=-1, keepdims=True)
    return xc * lax.rsqrt(var + LN_EPS) * g + b


def _dot_t(at, b):
    return lax.dot_general(at, b, (((0,), (0,)), ((), ())), preferred_element_type=F32)


def _merge_kernel(x_ref, oa_ref, ob_ref, wg_ref, bg_ref, wpa_ref, wpb_ref, wo_ref, sub_ref,
                  lq1_ref, lk1_ref, lq2_ref, lk2_ref, g_ref, b_ref, h_ref, ht_ref):
    x = x_ref[0]
    xb = x.astype(BF16)
    tm = x.shape[0]
    lam = (jnp.exp(jnp.sum(lq1_ref[...] * lk1_ref[...], axis=-1, keepdims=True))
           - jnp.exp(jnp.sum(lq2_ref[...] * lk2_ref[...], axis=-1, keepdims=True)) + LAMBDA_INIT)
    gl = jnp.dot(xb, wg_ref[...], preferred_element_type=F32) + bg_ref[...]
    gate_a = jax.nn.sigmoid(gl[:, :D_MODEL])
    gate_b = jax.nn.sigmoid(gl[:, D_MODEL:])

    pa = _dot_t(oa_ref[0].reshape(COLS_A_Q, tm), wpa_ref[...])
    ybt = []
    for h in range(B_HEADS):
        y = ob_ref[0, 2 * h].astype(F32) - lam * ob_ref[0, 2 * h + 1].astype(F32)
        y = y * lax.rsqrt(jnp.mean(y * y, axis=0, keepdims=True) + EPS) * sub_ref[...]
        ybt.append((y * (1.0 - LAMBDA_INIT)).astype(BF16))
    pb = _dot_t(jnp.concatenate(ybt, axis=0), wpb_ref[...])
    merged = gate_a * pa + gate_b * pb
    pre = DN_ALPHA * x + jnp.dot(merged.astype(BF16), wo_ref[...], preferred_element_type=F32)
    h_out = _layer_norm(pre, g_ref[...], b_ref[...])
    h_ref[0] = h_out
    ht_ref[...] = h_out.T.astype(ht_ref.dtype)


def _merge(x, oat, obt, wg, bg, wpa, wpb, wo, sub, lams, g, b, tm):
    bsz, seq, _ = x.shape
    nblk = seq // tm
    const = lambda bb, i: (0, 0)
    vec = lambda w: pl.BlockSpec((1, w), const)
    return pl.pallas_call(
        _merge_kernel,
        grid=(bsz, nblk),
        in_specs=[
            pl.BlockSpec((1, tm, D_MODEL), lambda bb, i: (bb, i, 0)),
            pl.BlockSpec((1, A_Q_HEADS, HEAD_DIM, tm), lambda bb, i: (bb, 0, 0, i)),
            pl.BlockSpec((1, 2 * B_HEADS, B_V_DIM, tm), lambda bb, i: (bb, 0, 0, i)),
            pl.BlockSpec((D_MODEL, 2 * D_MODEL), const),
            vec(2 * D_MODEL),
            pl.BlockSpec((COLS_A_Q, D_MODEL), const),
            pl.BlockSpec((COLS_B_V, D_MODEL), const),
            pl.BlockSpec((D_MODEL, D_MODEL), const),
            pl.BlockSpec((B_V_DIM, 1), const),
            vec(HEAD_DIM), vec(HEAD_DIM), vec(HEAD_DIM), vec(HEAD_DIM),
            vec(D_MODEL), vec(D_MODEL),
        ],
        out_specs=(
            pl.BlockSpec((1, tm, D_MODEL), lambda bb, i: (bb, i, 0)),
            pl.BlockSpec((D_MODEL, tm), lambda bb, i: (0, bb * nblk + i)),
        ),
        out_shape=(
            jax.ShapeDtypeStruct((bsz, seq, D_MODEL), F32),
            jax.ShapeDtypeStruct((D_MODEL, bsz * seq), BF16),
        ),
        compiler_params=_cparams(("parallel", "parallel")),
        name="merge",
    )(x, oat, obt, wg, bg, wpa, wpb, wo, sub, *lams, g, b)


_NO_RANK = 32.0


def _top_values(s, n, with_rank=False):
    vals = []
    rank = jnp.full(s.shape, _NO_RANK, F32) if with_rank else None
    for i in range(n):
        m = jnp.max(s, axis=0, keepdims=True)
        vals.append(m)
        hit = s == m
        if with_rank:
            rank = jnp.where(hit, float(i), rank)
        s = jnp.where(hit, -jnp.inf, s)
    return (vals, rank) if with_rank else vals


def _route_kernel(ht_ref, wq_ref, keys_ref, cnt_ref, coef_ref, rank_ref, e2_ref):
    qt = jnp.dot(wq_ref[...], ht_ref[...], preferred_element_type=F32).astype(BF16)
    s1_all = jnp.dot(keys_ref[0, 0], qt[:PEER_HALF], preferred_element_type=F32)
    s2_all = jnp.dot(keys_ref[0, 1], qt[PEER_HALF:], preferred_element_type=F32)
    for g in range(s1_all.shape[1] // LANES):
        lanes = slice(g * LANES, (g + 1) * LANES)
        s1 = s1_all[:, lanes]
        s2 = s2_all[:, lanes]
        v1 = _top_values(s1, PEER_TOPK + 1)
        v2, rank2 = _top_values(s2, PEER_TOPK + 1, with_rank=True)
        pair = {(k, l): v1[k] + v2[l] for k, l in _CAND_PAIRS}
        top = _top_values(jnp.concatenate([pair[p] for p in _CAND_PAIRS], axis=0), PEER_TOPK + 1)
        thr = 0.5 * (top[PEER_TOPK - 1] + top[PEER_TOPK])
        ev1 = [jnp.exp(v - v1[0]) for v in v1]
        ev2 = [jnp.exp(v - v2[0]) for v in v2]
        z = jnp.zeros_like(thr)
        pairs = [jnp.zeros_like(thr) for _ in v1]
        for k, l in _CAND_PAIRS:
            sel = pair[(k, l)] > thr
            z = z + jnp.where(sel, ev1[k] * ev2[l], 0.0)
            pairs[k] = pairs[k] + jnp.where(sel, 1.0, 0.0)
        cnt = jnp.zeros_like(s1)
        for k, v in enumerate(v1):
            cnt = jnp.where(s1 == v, pairs[k], cnt)
        cnt_ref[0, :, lanes] = cnt
        coef_ref[0, :, lanes] = jnp.exp(s1 - v1[0]) / z
        rank_ref[0, :, lanes] = rank2.astype(rank_ref.dtype)
        e2_ref[0, :, lanes] = jnp.exp(s2 - v2[0]).astype(e2_ref.dtype)


def _route(ht, wq_t, keys, tm):
    ntok = ht.shape[1]
    shape = (PEER_HEADS, PEER_N_KEYS, ntok)
    ospec = pl.BlockSpec((1, PEER_N_KEYS, tm), lambda t, h: (h, 0, t))
    return pl.pallas_call(
        _route_kernel,
        grid=(ntok // tm, PEER_HEADS),
        in_specs=[
            pl.BlockSpec((D_MODEL, tm), lambda t, h: (0, t)),
            pl.BlockSpec((2 * PEER_HALF, D_MODEL), lambda t, h: (h, 0)),
            pl.BlockSpec((1, 2, PEER_N_KEYS, PEER_HALF), lambda t, h: (h, 0, 0, 0)),
        ],
        out_specs=(ospec, ospec, ospec, ospec),
        out_shape=(jax.ShapeDtypeStruct(shape, F32), jax.ShapeDtypeStruct(shape, F32),
                   jax.ShapeDtypeStruct(shape, BF16), jax.ShapeDtypeStruct(shape, BF16)),
        compiler_params=_cparams(("parallel", "parallel")),
        name="route",
    )(ht, wq_t, keys)


def _gelu(a):
    return 0.5 * a * (1.0 + lax.erf(a * (2.0 ** -0.5)))


def _peer_kernel(ht_ref, u_ref, vt_ref, cnt_ref, coef_ref, rank_ref, e2_ref, h_ref, g_ref, b_ref,
                 o_ref, yt_ref, *, rows_per_step):
    e = pl.program_id(1)

    @pl.when(e == 0)
    def _():
        yt_ref[...] = jnp.zeros_like(yt_ref)

    at = jnp.dot(u_ref[...], ht_ref[...], preferred_element_type=F32)
    tm = at.shape[1]
    pieces = []
    for r in range(rows_per_step):
        gate = jnp.zeros((PEER_N_KEYS, tm), BF16)
        for h in range(PEER_HEADS):
            cnt = jnp.broadcast_to(cnt_ref[h, r:r + 1, :], (BF16_ROWS, tm)).astype(BF16)
            coef = jnp.broadcast_to(coef_ref[h, r:r + 1, :], (BF16_ROWS, tm)).astype(BF16)
            cnt = jnp.tile(cnt, (PEER_N_KEYS // BF16_ROWS, 1))
            coef = jnp.tile(coef, (PEER_N_KEYS // BF16_ROWS, 1))
            gate = gate + jnp.where(rank_ref[h] < cnt, e2_ref[h] * coef, jnp.zeros_like(coef))
        a = at[r * PEER_N_KEYS:(r + 1) * PEER_N_KEYS, :]
        pieces.append(_gelu(a).astype(BF16) * gate)
    wt = jnp.concatenate(pieces, axis=0)
    yt_ref[...] += jnp.dot(vt_ref[...], wt, preferred_element_type=F32)

    @pl.when(e == pl.num_programs(1) - 1)
    def _():
        pre = DN_ALPHA * h_ref[...] + yt_ref[...].T
        o_ref[...] = _layer_norm(pre, g_ref[...], b_ref[...])


def _peer(ht, u, vt, cnt, coef, rank, e2, h, g, b, tm, eb):
    ntok = ht.shape[1]
    rows = eb // PEER_N_KEYS
    full = pl.BlockSpec((PEER_HEADS, PEER_N_KEYS, tm), lambda t, e: (0, 0, t))
    part = pl.BlockSpec((PEER_HEADS, rows, tm), lambda t, e: (0, e, t))
    vec = pl.BlockSpec((1, D_MODEL), lambda t, e: (0, 0))
    return pl.pallas_call(
        functools.partial(_peer_kernel, rows_per_step=rows),
        grid=(ntok // tm, PEER_N_EXPERTS // eb),
        in_specs=[
            pl.BlockSpec((D_MODEL, tm), lambda t, e: (0, t)),
            pl.BlockSpec((eb, D_MODEL), lambda t, e: (e, 0)),
            pl.BlockSpec((D_MODEL, eb), lambda t, e: (0, e)),
            part, part, full, full,
            pl.BlockSpec((tm, D_MODEL), lambda t, e: (t, 0)),
            vec, vec,
        ],
        out_specs=pl.BlockSpec((tm, D_MODEL), lambda t, e: (t, 0)),
        out_shape=jax.ShapeDtypeStruct((ntok, D_MODEL), F32),
        scratch_shapes=[pltpu.VMEM((D_MODEL, tm), F32)],
        compiler_params=_cparams(("parallel", "arbitrary")),
        name="peer",
    )(ht, u, vt, cnt, coef, rank, e2, h, g, b)


def _hybrid_block(x, w_in, b_gate, q_norm_a, k_norm_a, lambda_q1, lambda_k1, lambda_q2, lambda_k2,
                  subln_b, w_proj_a, w_proj_b, w_out, ln1_g, ln1_b, w_query, sub_keys,
                  u_table, v_table, ln2_g, ln2_b, *, tk, tq_a, tq_b, tm_merge, tm_route, tm_peer, eb):
    bsz, seq, _ = x.shape
    row = lambda a: a.reshape(1, -1).astype(F32)
    two = lambda a: jnp.concatenate([row(a), row(a)], axis=-1)

    qat, ka, vat, qbt, kb, vbt = _prep(x, w_in[:, :COLS_QKV].astype(BF16), two(q_norm_a), two(k_norm_a),
                                       _rope_tables(seq), tk)

    nchunk = seq // tk
    oat = _flash(qat.reshape(bsz * A_KV_HEADS, A_GROUP, HEAD_DIM, seq),
                 ka.reshape(bsz * A_KV_HEADS, 1, seq, HEAD_DIM),
                 vat.reshape(bsz * A_KV_HEADS, 1, nchunk, HEAD_DIM + BF16_ROWS, tk),
                 v_of=lambda n: n, tq=tq_a, dv=HEAD_DIM)
    obt = _flash(qbt.reshape(bsz * 2 * B_HEADS, 1, HEAD_DIM, seq),
                 kb.reshape(bsz * 2 * B_HEADS, 1, seq, HEAD_DIM),
                 vbt.reshape(bsz * B_HEADS, 1, nchunk, B_V_DIM + BF16_ROWS, tk),
                 v_of=lambda n: n // 2, tq=tq_b, dv=B_V_DIM)

    h, ht = _merge(x, oat.reshape(bsz, A_Q_HEADS, HEAD_DIM, seq), obt.reshape(bsz, 2 * B_HEADS, B_V_DIM, seq),
                   w_in[:, COLS_QKV:].astype(BF16), row(b_gate), w_proj_a.astype(BF16),
                   w_proj_b.astype(BF16), w_out.astype(BF16), subln_b.reshape(-1, 1).astype(F32),
                   (row(lambda_q1), row(lambda_k1), row(lambda_q2), row(lambda_k2)),
                   row(ln1_g), row(ln1_b), tm_merge)

    cnt, coef, rank, e2 = _route(ht, w_query.T.astype(BF16), sub_keys.astype(BF16), tm_route)
    out = _peer(ht, u_table.astype(BF16), v_table.T.astype(BF16), cnt, coef, rank, e2,
                h.reshape(bsz * seq, D_MODEL), row(ln2_g), row(ln2_b), tm_peer, eb)
    return out.reshape(bsz, seq, D_MODEL)


def kernel(x, w_in, b_gate, q_norm_a, k_norm_a, lambda_q1, lambda_k1, lambda_q2, lambda_k2, subln_b,
           w_proj_a, w_proj_b, w_out, ln1_g, ln1_b, w_query, sub_keys, u_table, v_table, ln2_g, ln2_b):
    return _hybrid_block(
        x, w_in[0], b_gate[0], q_norm_a[0], k_norm_a[0], lambda_q1[0], lambda_k1[0], lambda_q2[0],
        lambda_k2[0], subln_b[0], w_proj_a[0], w_proj_b[0], w_out[0], ln1_g[0], ln1_b[0], w_query[0],
        sub_keys[0], u_table[0], v_table[0], ln2_g[0], ln2_b[0],
        tk=512, tq_a=256, tq_b=1024, tm_merge=512, tm_route=512, tm_peer=512, eb=1024)
```

```python
import functools
import math

import jax
import jax.numpy as jnp
from jax import lax
from jax.experimental import pallas as pl
from jax.experimental.pallas import tpu as pltpu

F32 = jnp.float32
BF16 = jnp.bfloat16

D_MODEL = 1024
HEAD_DIM = 64
A_Q_HEADS = 8
A_KV_HEADS = 2
A_GROUP = A_Q_HEADS // A_KV_HEADS
B_HEADS = 4
B_V_DIM = 2 * HEAD_DIM
GRID_W = 64
ROPE_THETA = 10000.0
EPS = 1e-6
LN_EPS = 1e-5
PEER_HEADS = 8
PEER_N_KEYS = 128
PEER_N_EXPERTS = PEER_N_KEYS * PEER_N_KEYS
PEER_HALF = 128
PEER_TOPK = 16
DEPTH = 1
DN_ALPHA = (2.0 * DEPTH) ** 0.25
LAMBDA_INIT = 0.8 - 0.6 * math.exp(-0.3 * 0)

COLS_A_Q = A_Q_HEADS * HEAD_DIM
COLS_A_KV = A_KV_HEADS * HEAD_DIM
COLS_B_QK = B_HEADS * 2 * HEAD_DIM
COLS_B_V = B_HEADS * B_V_DIM
COLS_QKV = COLS_A_Q + 2 * COLS_A_KV + 2 * COLS_B_QK + COLS_B_V

LANES = 128
BF16_ROWS = 16
VMEM_LIMIT = 48 * 1024 * 1024
LOG2E = math.log2(math.e)

_CAND_PAIRS = tuple((k, l) for k in range(PEER_TOPK + 1) for l in range(PEER_TOPK + 1)
                    if (k + 1) * (l + 1) <= PEER_TOPK + 1)


def _cparams(sem):
    return pltpu.CompilerParams(dimension_semantics=sem, vmem_limit_bytes=VMEM_LIMIT)


def _rope_tables(seq):
    t = jnp.arange(seq, dtype=jnp.int32)

    def tab(pos, dim):
        freqs = ROPE_THETA ** (-jnp.arange(0, dim, 2, dtype=F32) / dim)
        ang = pos.astype(F32)[:, None] * freqs[None, :]
        return jnp.cos(ang), jnp.sin(ang)

    cr, sr = tab(t // GRID_W, HEAD_DIM // 2)
    cc, sc = tab(t % GRID_W, HEAD_DIM // 2)
    ct, st = tab(t, HEAD_DIM)
    z16 = jnp.zeros_like(sr)
    z32 = jnp.zeros_like(st)
    cos_a = jnp.concatenate([cr, cr, cc, cc], axis=-1)
    up_a = jnp.concatenate([-sr, z16, -sc, z16], axis=-1)
    dn_a = jnp.concatenate([z16, sr, z16, sc], axis=-1)
    cos_b = jnp.concatenate([ct, ct], axis=-1)
    up_b = jnp.concatenate([-st, z32], axis=-1)
    dn_b = jnp.concatenate([z32, st], axis=-1)
    two = lambda a: jnp.concatenate([a, a], axis=-1)
    return tuple(two(a) for a in (cos_a, up_a, dn_a, cos_b, up_b, dn_b))


def _rotate(x, cos, up, dn, half):
    n = x.shape[-1]
    return x * cos + pltpu.roll(x, n - half, 1) * up + pltpu.roll(x, half, 1) * dn


def _prep_kernel(x_ref, w_ref, gq_ref, gk_ref, ca_ref, ua_ref, da_ref, cb_ref, ub_ref, db_ref,
                 gsum_ref, qa_ref, ka_ref, va_ref, qb_ref, kb_ref, vb_ref):
    xb = x_ref[0].astype(BF16)
    tm = xb.shape[0]
    q_scale = HEAD_DIM ** -0.5 * LOG2E
    ones_rows = (lax.broadcasted_iota(jnp.int32, (BF16_ROWS, tm), 0) == 0).astype(BF16)

    def proj(c0, width):
        return jnp.dot(xb, w_ref[:, c0:c0 + width], preferred_element_type=F32)

    def norm_rope_a(y, gain):
        ss = jnp.dot((y * y).astype(BF16), gsum_ref[...], preferred_element_type=F32)
        y = y * lax.rsqrt(ss * (1.0 / HEAD_DIM) + EPS) * gain
        return _rotate(y, ca_ref[...], ua_ref[...], da_ref[...], HEAD_DIM // 4)

    def rope_b(y):
        return _rotate(y, cb_ref[...], ub_ref[...], db_ref[...], HEAD_DIM // 2)

    def store_q(ref, y, c):
        yt = (y * q_scale).T.astype(ref.dtype)
        ref[0, 2 * c] = yt[:HEAD_DIM]
        ref[0, 2 * c + 1] = yt[HEAD_DIM:]

    def store_k(ref, y, c):
        y = y.astype(ref.dtype)
        ref[0, 2 * c] = y[:, :HEAD_DIM]
        ref[0, 2 * c + 1] = y[:, HEAD_DIM:]

    c0 = 0
    qa = proj(c0, COLS_A_Q)
    for c in range(COLS_A_Q // LANES):
        store_q(qa_ref, norm_rope_a(qa[:, c * LANES:(c + 1) * LANES], gq_ref[...]), c)
    c0 += COLS_A_Q
    store_k(ka_ref, norm_rope_a(proj(c0, COLS_A_KV), gk_ref[...]), 0)
    c0 += COLS_A_KV
    vat = proj(c0, COLS_A_KV).T.astype(va_ref.dtype)
    for h in range(A_KV_HEADS):
        va_ref[0, h, 0, :HEAD_DIM] = vat[h * HEAD_DIM:(h + 1) * HEAD_DIM]
        va_ref[0, h, 0, HEAD_DIM:] = ones_rows
    c0 += COLS_A_KV
    qb = proj(c0, COLS_B_QK)
    for c in range(COLS_B_QK // LANES):
        store_q(qb_ref, rope_b(qb[:, c * LANES:(c + 1) * LANES]), c)
    c0 += COLS_B_QK
    kb = proj(c0, COLS_B_QK)
    for c in range(COLS_B_QK // LANES):
        store_k(kb_ref, rope_b(kb[:, c * LANES:(c + 1) * LANES]), c)
    c0 += COLS_B_QK
    vb = proj(c0, COLS_B_V)
    for h in range(B_HEADS):
        vb_ref[0, h, 0, :B_V_DIM] = vb[:, h * B_V_DIM:(h + 1) * B_V_DIM].T.astype(vb_ref.dtype)
        vb_ref[0, h, 0, B_V_DIM:] = ones_rows


def _prep(x, w_qkv, gq, gk, tables, tm):
    bsz, seq, _ = x.shape
    nblk = seq // tm
    gsum = jnp.kron(jnp.eye(LANES // HEAD_DIM, dtype=F32), jnp.ones((HEAD_DIM, HEAD_DIM), F32)).astype(BF16)
    tok = lambda b, i: (i, 0)
    const = lambda b, i: (0, 0)
    qt_out = lambda nh: pl.BlockSpec((1, nh, HEAD_DIM, tm), lambda b, i: (b, 0, 0, i))
    k_out = lambda nh: pl.BlockSpec((1, nh, tm, HEAD_DIM), lambda b, i: (b, 0, i, 0))
    vt_out = lambda nh, r: pl.BlockSpec((1, nh, 1, r, tm), lambda b, i: (b, 0, i, 0, 0))
    rows_a = HEAD_DIM + BF16_ROWS
    rows_b = B_V_DIM + BF16_ROWS
    out_shape = (
        jax.ShapeDtypeStruct((bsz, A_Q_HEADS, HEAD_DIM, seq), BF16),
        jax.ShapeDtypeStruct((bsz, A_KV_HEADS, seq, HEAD_DIM), BF16),
        jax.ShapeDtypeStruct((bsz, A_KV_HEADS, nblk, rows_a, tm), BF16),
        jax.ShapeDtypeStruct((bsz, 2 * B_HEADS, HEAD_DIM, seq), BF16),
        jax.ShapeDtypeStruct((bsz, 2 * B_HEADS, seq, HEAD_DIM), BF16),
        jax.ShapeDtypeStruct((bsz, B_HEADS, nblk, rows_b, tm), BF16),
    )
    return pl.pallas_call(
        _prep_kernel,
        grid=(bsz, nblk),
        in_specs=[
            pl.BlockSpec((1, tm, D_MODEL), lambda b, i: (b, i, 0)),
            pl.BlockSpec((D_MODEL, COLS_QKV), const),
            pl.BlockSpec((1, LANES), const),
            pl.BlockSpec((1, LANES), const),
        ] + [pl.BlockSpec((tm, LANES), tok)] * 6 + [pl.BlockSpec((LANES, LANES), const)],
        out_specs=(qt_out(A_Q_HEADS), k_out(A_KV_HEADS), vt_out(A_KV_HEADS, rows_a),
                   qt_out(2 * B_HEADS), k_out(2 * B_HEADS), vt_out(B_HEADS, rows_b)),
        out_shape=out_shape,
        compiler_params=_cparams(("parallel", "parallel")),
        name="prep",
    )(x, w_qkv, gq, gk, *tables, gsum)


def _flash_kernel(qt_ref, k_ref, vt_ref, o_ref, qcat_ref, acc_ref, m_ref, s0_ref, s1_ref, c0_ref, c1_ref,
                  *, dv):
    group, _, tq = qt_ref.shape[1:]
    nchunk, _, tk = vt_ref.shape[2:]
    for r in range(group):
        qcat_ref[:, r * tq:(r + 1) * tq] = qt_ref[0, r]
    acc_ref[...] = jnp.zeros_like(acc_ref)
    m_ref[...] = jnp.full_like(m_ref, -jnp.inf)

    def scores(s_ref, c_ref, j):
        start = pl.multiple_of(j * tk, tk)
        k = k_ref[0, 0, pl.ds(start, tk), :]
        st = jnp.dot(k, qcat_ref[...], preferred_element_type=F32)
        s_ref[:, :st.shape[1]] = st
        c_ref[...] = jnp.max(st, axis=0, keepdims=True)

    def update(s_ref, c_ref, j):
        m_prev = m_ref[...]
        m_new = jnp.maximum(m_prev, c_ref[...])
        pt = jnp.exp2(s_ref[:, :m_new.shape[1]] - m_new).astype(BF16)
        alpha = jnp.exp2(m_prev - m_new)
        acc_ref[...] = alpha * acc_ref[...] + jnp.dot(vt_ref[0, 0, j], pt, preferred_element_type=F32)
        m_ref[...] = m_new

    scores(s0_ref, c0_ref, 0)

    def body(i, carry):
        j = 2 * i
        scores(s1_ref, c1_ref, j + 1)
        update(s0_ref, c0_ref, j)
        scores(s0_ref, c0_ref, j + 2)
        update(s1_ref, c1_ref, j + 1)
        return carry

    lax.fori_loop(0, nchunk // 2 - 1, body, 0)
    scores(s1_ref, c1_ref, nchunk - 1)
    update(s0_ref, c0_ref, nchunk - 2)
    update(s1_ref, c1_ref, nchunk - 1)
    acc = acc_ref[...]
    ot = (acc[:dv] / acc[dv:dv + 1]).astype(o_ref.dtype)
    for r in range(group):
        o_ref[0, r] = ot[:, r * tq:(r + 1) * tq]


def _flash(qt, k, vt, *, v_of, tq, dv, s_pad):
    n, group, _, seq = qt.shape
    nchunk, rows, tk = vt.shape[2:]
    return pl.pallas_call(
        functools.partial(_flash_kernel, dv=dv),
        grid=(n, seq // tq),
        in_specs=[
            pl.BlockSpec((1, group, HEAD_DIM, tq), lambda b, i: (b, 0, 0, i)),
            pl.BlockSpec((1, 1, seq, HEAD_DIM), lambda b, i: (b, 0, 0, 0)),
            pl.BlockSpec((1, 1, nchunk, rows, tk), lambda b, i: (v_of(b), 0, 0, 0, 0)),
        ],
        out_specs=pl.BlockSpec((1, group, dv, tq), lambda b, i: (b, 0, 0, i)),
        out_shape=jax.ShapeDtypeStruct((n, group, dv, seq), BF16),
        scratch_shapes=[pltpu.VMEM((HEAD_DIM, group * tq), BF16),
                        pltpu.VMEM((rows, group * tq), F32),
                        pltpu.VMEM((1, group * tq), F32),
                        pltpu.VMEM((tk, group * tq + s_pad), F32),
                        pltpu.VMEM((tk, group * tq + s_pad), F32),
                        pltpu.VMEM((1, group * tq), F32),
                        pltpu.VMEM((1, group * tq), F32)],
        compiler_params=_cparams(("parallel", "parallel")),
        name="flash_dv%d" % dv,
    )(qt, k, vt)


def _layer_norm(x, g, b):
    mu = jnp.mean(x, axis=-1, keepdims=True)
    xc = x - mu
    var = jnp.mean(xc * xc, axis=-1, keepdims=True)
    return xc * lax.rsqrt(var + LN_EPS) * g + b


def _dot_t(at, b):
    return lax.dot_general(at, b, (((0,), (0,)), ((), ())), preferred_element_type=F32)


def _merge_kernel(x_ref, oa_ref, ob_ref, wg_ref, bg_ref, wpa_ref, wpb_ref, wo_ref, sub_ref,
                  lq1_ref, lk1_ref, lq2_ref, lk2_ref, g_ref, b_ref, h_ref, ht_ref):
    x = x_ref[0]
    xb = x.astype(BF16)
    tm = x.shape[0]
    lam = (jnp.exp(jnp.sum(lq1_ref[...] * lk1_ref[...], axis=-1, keepdims=True))
           - jnp.exp(jnp.sum(lq2_ref[...] * lk2_ref[...], axis=-1, keepdims=True)) + LAMBDA_INIT)
    gl = jnp.dot(xb, wg_ref[...], preferred_element_type=F32) + bg_ref[...]
    gate_a = jax.nn.sigmoid(gl[:, :D_MODEL])
    gate_b = jax.nn.sigmoid(gl[:, D_MODEL:])

    pa = _dot_t(oa_ref[0].reshape(COLS_A_Q, tm), wpa_ref[...])
    ybt = []
    for h in range(B_HEADS):
        y = ob_ref[0, 2 * h].astype(F32) - lam * ob_ref[0, 2 * h + 1].astype(F32)
        y = y * lax.rsqrt(jnp.mean(y * y, axis=0, keepdims=True) + EPS) * sub_ref[...]
        ybt.append((y * (1.0 - LAMBDA_INIT)).astype(BF16))
    pb = _dot_t(jnp.concatenate(ybt, axis=0), wpb_ref[...])
    merged = gate_a * pa + gate_b * pb
    pre = DN_ALPHA * x + jnp.dot(merged.astype(BF16), wo_ref[...], preferred_element_type=F32)
    h_out = _layer_norm(pre, g_ref[...], b_ref[...])
    h_ref[0] = h_out
    ht_ref[...] = h_out.T.astype(ht_ref.dtype)


def _merge(x, oat, obt, wg, bg, wpa, wpb, wo, sub, lams, g, b, tm):
    bsz, seq, _ = x.shape
    nblk = seq // tm
    const = lambda bb, i: (0, 0)
    vec = lambda w: pl.BlockSpec((1, w), const)
    return pl.pallas_call(
        _merge_kernel,
        grid=(bsz, nblk),
        in_specs=[
            pl.BlockSpec((1, tm, D_MODEL), lambda bb, i: (bb, i, 0)),
            pl.BlockSpec((1, A_Q_HEADS, HEAD_DIM, tm), lambda bb, i: (bb, 0, 0, i)),
            pl.BlockSpec((1, 2 * B_HEADS, B_V_DIM, tm), lambda bb, i: (bb, 0, 0, i)),
            pl.BlockSpec((D_MODEL, 2 * D_MODEL), const),
            vec(2 * D_MODEL),
            pl.BlockSpec((COLS_A_Q, D_MODEL), const),
            pl.BlockSpec((COLS_B_V, D_MODEL), const),
            pl.BlockSpec((D_MODEL, D_MODEL), const),
            pl.BlockSpec((B_V_DIM, 1), const),
            vec(HEAD_DIM), vec(HEAD_DIM), vec(HEAD_DIM), vec(HEAD_DIM),
            vec(D_MODEL), vec(D_MODEL),
        ],
        out_specs=(
            pl.BlockSpec((1, tm, D_MODEL), lambda bb, i: (bb, i, 0)),
            pl.BlockSpec((D_MODEL, tm), lambda bb, i: (0, bb * nblk + i)),
        ),
        out_shape=(
            jax.ShapeDtypeStruct((bsz, seq, D_MODEL), F32),
            jax.ShapeDtypeStruct((D_MODEL, bsz * seq), BF16),
        ),
        compiler_params=_cparams(("parallel", "parallel")),
        name="merge",
    )(x, oat, obt, wg, bg, wpa, wpb, wo, sub, *lams, g, b)


_NO_RANK = 32.0


SUBLANES = 8


def _sort_network(n):
    pairs, p = [], 1
    while p < n:
        k = p
        while k >= 1:
            for j in range(k % p, n - k, 2 * k):
                for i in range(min(k, n - j - k)):
                    if (i + j) // (2 * p) == (i + j + k) // (2 * p):
                        pairs.append((i + j, i + j + k))
            k //= 2
        p *= 2
    return pairs


def _top_values(s, n):
    vals = []
    if s.shape[0] != PEER_N_KEYS:
        for _ in range(n):
            m = jnp.max(s, axis=0, keepdims=True)
            vals.append(m)
            s = jnp.where(s == m, -jnp.inf, s)
        return vals
    rows = [s[r * SUBLANES:(r + 1) * SUBLANES] for r in range(s.shape[0] // SUBLANES)]
    for a, b in _sort_network(len(rows)):
        rows[a], rows[b] = jnp.maximum(rows[a], rows[b]), jnp.minimum(rows[a], rows[b])
    for i in range(n):
        m = jnp.max(rows[0], axis=0, keepdims=True)
        vals.append(m)
        hit = rows[0] == m
        depth = min(len(rows), n - i)
        below = rows[1:depth + 1]
        below = below + [jnp.full_like(rows[0], -jnp.inf)] * (depth - len(below))
        rows = [jnp.where(hit, below[k], rows[k]) for k in range(depth)]
    return vals


def _rank_codes(s, vals):
    rank = jnp.full(s.shape, _NO_RANK, F32)
    for i, v in enumerate(vals):
        rank = jnp.where(s == v, float(i), rank)
    return rank


def _route_kernel(ht_ref, wq_ref, keys_ref, cnt_ref, coef_ref, rank_ref, e2_ref):
    qt = jnp.dot(wq_ref[...], ht_ref[...], preferred_element_type=F32).astype(BF16)
    s1_all = jnp.dot(keys_ref[0, 0], qt[:PEER_HALF], preferred_element_type=F32)
    s2_all = jnp.dot(keys_ref[0, 1], qt[PEER_HALF:], preferred_element_type=F32)
    for g in range(s1_all.shape[1] // LANES):
        lanes = slice(g * LANES, (g + 1) * LANES)
        s1 = s1_all[:, lanes]
        s2 = s2_all[:, lanes]
        v1 = _top_values(s1, PEER_TOPK + 1)
        v2 = _top_values(s2, PEER_TOPK + 1)
        rank2 = _rank_codes(s2, v2)
        pair = {(k, l): v1[k] + v2[l] for k, l in _CAND_PAIRS}
        top = _top_values(jnp.concatenate([pair[p] for p in _CAND_PAIRS], axis=0), PEER_TOPK + 1)
        thr = 0.5 * (top[PEER_TOPK - 1] + top[PEER_TOPK])
        ev1 = [jnp.exp(v - v1[0]) for v in v1]
        ev2 = [jnp.exp(v - v2[0]) for v in v2]
        z = jnp.zeros_like(thr)
        pairs = [jnp.zeros_like(thr) for _ in v1]
        for k, l in _CAND_PAIRS:
            sel = pair[(k, l)] > thr
            z = z + jnp.where(sel, ev1[k] * ev2[l], 0.0)
            pairs[k] = pairs[k] + jnp.where(sel, 1.0, 0.0)
        cnt = jnp.zeros_like(s1)
        for k, v in enumerate(v1):
            cnt = jnp.where(s1 == v, pairs[k], cnt)
        cnt_ref[0, :, lanes] = cnt
        coef_ref[0, :, lanes] = jnp.exp(s1 - v1[0]) / z
        rank_ref[0, :, lanes] = rank2.astype(rank_ref.dtype)
        e2_ref[0, :, lanes] = jnp.exp(s2 - v2[0]).astype(e2_ref.dtype)


def _route(ht, wq_t, keys, tm):
    ntok = ht.shape[1]
    shape = (PEER_HEADS, PEER_N_KEYS, ntok)
    ospec = pl.BlockSpec((1, PEER_N_KEYS, tm), lambda t, h: (h, 0, t))
    return pl.pallas_call(
        _route_kernel,
        grid=(ntok // tm, PEER_HEADS),
        in_specs=[
            pl.BlockSpec((D_MODEL, tm), lambda t, h: (0, t)),
            pl.BlockSpec((2 * PEER_HALF, D_MODEL), lambda t, h: (h, 0)),
            pl.BlockSpec((1, 2, PEER_N_KEYS, PEER_HALF), lambda t, h: (h, 0, 0, 0)),
        ],
        out_specs=(ospec, ospec, ospec, ospec),
        out_shape=(jax.ShapeDtypeStruct(shape, F32), jax.ShapeDtypeStruct(shape, F32),
                   jax.ShapeDtypeStruct(shape, BF16), jax.ShapeDtypeStruct(shape, BF16)),
        compiler_params=_cparams(("parallel", "parallel")),
        name="route",
    )(ht, wq_t, keys)


def _gelu(a):
    return 0.5 * a * (1.0 + lax.erf(a * (2.0 ** -0.5)))


def _peer_kernel(ht_ref, u_ref, vt_ref, cnt_ref, coef_ref, rank_ref, e2_ref, h_ref, g_ref, b_ref,
                 o_ref, yt_ref, *, rows_per_step):
    e = pl.program_id(1)

    @pl.when(e == 0)
    def _():
        yt_ref[...] = jnp.zeros_like(yt_ref)

    at = jnp.dot(u_ref[...], ht_ref[...], preferred_element_type=F32)
    tm = at.shape[1]
    pieces = []
    for r in range(rows_per_step):
        gate = jnp.zeros((PEER_N_KEYS, tm), BF16)
        for h in range(PEER_HEADS):
            cnt = jnp.broadcast_to(cnt_ref[h, r:r + 1, :], (BF16_ROWS, tm)).astype(BF16)
            coef = jnp.broadcast_to(coef_ref[h, r:r + 1, :], (BF16_ROWS, tm)).astype(BF16)
            cnt = jnp.tile(cnt, (PEER_N_KEYS // BF16_ROWS, 1))
            coef = jnp.tile(coef, (PEER_N_KEYS // BF16_ROWS, 1))
            gate = gate + jnp.where(rank_ref[h] < cnt, e2_ref[h] * coef, jnp.zeros_like(coef))
        a = at[r * PEER_N_KEYS:(r + 1) * PEER_N_KEYS, :]
        pieces.append(_gelu(a.astype(BF16)) * gate)
    wt = jnp.concatenate(pieces, axis=0)
    yt_ref[...] += jnp.dot(vt_ref[...], wt, preferred_element_type=F32)

    @pl.when(e == pl.num_programs(1) - 1)
    def _():
        pre = DN_ALPHA * h_ref[...] + yt_ref[...].T
        o_ref[...] = _layer_norm(pre, g_ref[...], b_ref[...])


def _peer(ht, u, vt, cnt, coef, rank, e2, h, g, b, tm, eb):
    ntok = ht.shape[1]
    rows = eb // PEER_N_KEYS
    full = pl.BlockSpec((PEER_HEADS, PEER_N_KEYS, tm), lambda t, e: (0, 0, t))
    part = pl.BlockSpec((PEER_HEADS, rows, tm), lambda t, e: (0, e, t))
    vec = pl.BlockSpec((1, D_MODEL), lambda t, e: (0, 0))
    return pl.pallas_call(
        functools.partial(_peer_kernel, rows_per_step=rows),
        grid=(ntok // tm, PEER_N_EXPERTS // eb),
        in_specs=[
            pl.BlockSpec((D_MODEL, tm), lambda t, e: (0, t)),
            pl.BlockSpec((eb, D_MODEL), lambda t, e: (e, 0)),
            pl.BlockSpec((D_MODEL, eb), lambda t, e: (0, e)),
            part, part, full, full,
            pl.BlockSpec((tm, D_MODEL), lambda t, e: (t, 0)),
            vec, vec,
        ],
        out_specs=pl.BlockSpec((tm, D_MODEL), lambda t, e: (t, 0)),
        out_shape=jax.ShapeDtypeStruct((ntok, D_MODEL), F32),
        scratch_shapes=[pltpu.VMEM((D_MODEL, tm), F32)],
        compiler_params=_cparams(("parallel", "arbitrary")),
        name="peer",
    )(ht, u, vt, cnt, coef, rank, e2, h, g, b)


def _hybrid_block(x, w_in, b_gate, q_norm_a, k_norm_a, lambda_q1, lambda_k1, lambda_q2, lambda_k2,
                  subln_b, w_proj_a, w_proj_b, w_out, ln1_g, ln1_b, w_query, sub_keys,
                  u_table, v_table, ln2_g, ln2_b, *, tk, tq_a, tq_b, tm_merge, tm_route, tm_peer, eb):
    bsz, seq, _ = x.shape
    row = lambda a: a.reshape(1, -1).astype(F32)
    two = lambda a: jnp.concatenate([row(a), row(a)], axis=-1)

    qat, ka, vat, qbt, kb, vbt = _prep(x, w_in[:, :COLS_QKV].astype(BF16), two(q_norm_a), two(k_norm_a),
                                       _rope_tables(seq), tk)

    nchunk = seq // tk
    oat = _flash(qat.reshape(bsz * A_KV_HEADS, A_GROUP, HEAD_DIM, seq),
                 ka.reshape(bsz * A_KV_HEADS, 1, seq, HEAD_DIM),
                 vat.reshape(bsz * A_KV_HEADS, 1, nchunk, HEAD_DIM + BF16_ROWS, tk),
                 v_of=lambda n: n, tq=tq_a, dv=HEAD_DIM, s_pad=0)
    obt = _flash(qbt.reshape(bsz * 2 * B_HEADS, 1, HEAD_DIM, seq),
                 kb.reshape(bsz * 2 * B_HEADS, 1, seq, HEAD_DIM),
                 vbt.reshape(bsz * B_HEADS, 1, nchunk, B_V_DIM + BF16_ROWS, tk),
                 v_of=lambda n: n // 2, tq=tq_b, dv=B_V_DIM, s_pad=2 * LANES)

    h, ht = _merge(x, oat.reshape(bsz, A_Q_HEADS, HEAD_DIM, seq), obt.reshape(bsz, 2 * B_HEADS, B_V_DIM, seq),
                   w_in[:, COLS_QKV:].astype(BF16), row(b_gate), w_proj_a.astype(BF16),
                   w_proj_b.astype(BF16), w_out.astype(BF16), subln_b.reshape(-1, 1).astype(F32),
                   (row(lambda_q1), row(lambda_k1), row(lambda_q2), row(lambda_k2)),
                   row(ln1_g), row(ln1_b), tm_merge)

    cnt, coef, rank, e2 = _route(ht, w_query.T.astype(BF16), sub_keys.astype(BF16), tm_route)
    out = _peer(ht, u_table.astype(BF16), v_table.T.astype(BF16), cnt, coef, rank, e2,
                h.reshape(bsz * seq, D_MODEL), row(ln2_g), row(ln2_b), tm_peer, eb)
    return out.reshape(bsz, seq, D_MODEL)


def kernel(x, w_in, b_gate, q_norm_a, k_norm_a, lambda_q1, lambda_k1, lambda_q2, lambda_k2, subln_b,
           w_proj_a, w_proj_b, w_out, ln1_g, ln1_b, w_query, sub_keys, u_table, v_table, ln2_g, ln2_b):
    return _hybrid_block(
        x, w_in[0], b_gate[0], q_norm_a[0], k_norm_a[0], lambda_q1[0], lambda_k1[0], lambda_q2[0],
        lambda_k2[0], subln_b[0], w_proj_a[0], w_proj_b[0], w_out[0], ln1_g[0], ln1_b[0], w_query[0],
        sub_keys[0], u_table[0], v_table[0], ln2_g[0], ln2_b[0],
        tk=512, tq_a=256, tq_b=1024, tm_merge=512, tm_route=512, tm_peer=512, eb=1024)
```

```python
import functools
import math

import jax
import jax.numpy as jnp
from jax import lax
from jax.experimental import pallas as pl
from jax.experimental.pallas import tpu as pltpu

F32 = jnp.float32
BF16 = jnp.bfloat16

D_MODEL = 1024
HEAD_DIM = 64
A_Q_HEADS = 8
A_KV_HEADS = 2
A_GROUP = A_Q_HEADS // A_KV_HEADS
B_HEADS = 4
B_V_DIM = 2 * HEAD_DIM
GRID_W = 64
ROPE_THETA = 10000.0
EPS = 1e-6
LN_EPS = 1e-5
PEER_HEADS = 8
PEER_N_KEYS = 128
PEER_N_EXPERTS = PEER_N_KEYS * PEER_N_KEYS
PEER_HALF = 128
PEER_TOPK = 16
DEPTH = 1
DN_ALPHA = (2.0 * DEPTH) ** 0.25
LAMBDA_INIT = 0.8 - 0.6 * math.exp(-0.3 * 0)

COLS_A_Q = A_Q_HEADS * HEAD_DIM
COLS_A_KV = A_KV_HEADS * HEAD_DIM
COLS_B_QK = B_HEADS * 2 * HEAD_DIM
COLS_B_V = B_HEADS * B_V_DIM
COLS_QKV = COLS_A_Q + 2 * COLS_A_KV + 2 * COLS_B_QK + COLS_B_V

LANES = 128
BF16_ROWS = 16
VMEM_LIMIT = 48 * 1024 * 1024
LOG2E = math.log2(math.e)

_CAND_PAIRS = tuple((k, l) for k in range(PEER_TOPK + 1) for l in range(PEER_TOPK + 1)
                    if (k + 1) * (l + 1) <= PEER_TOPK + 1)


def _cparams(sem):
    return pltpu.CompilerParams(dimension_semantics=sem, vmem_limit_bytes=VMEM_LIMIT)


def _rope_tables(seq):
    t = jnp.arange(seq, dtype=jnp.int32)

    def tab(pos, dim):
        freqs = ROPE_THETA ** (-jnp.arange(0, dim, 2, dtype=F32) / dim)
        ang = pos.astype(F32)[:, None] * freqs[None, :]
        return jnp.cos(ang), jnp.sin(ang)

    cr, sr = tab(t // GRID_W, HEAD_DIM // 2)
    cc, sc = tab(t % GRID_W, HEAD_DIM // 2)
    ct, st = tab(t, HEAD_DIM)
    cos_a = jnp.concatenate([cr, cr, cc, cc] * 2, axis=-1)
    sin_a = jnp.concatenate([-sr, sr, -sc, sc] * 2, axis=-1)
    cos_b = jnp.concatenate([ct, ct] * 2, axis=-1)
    sin_b = jnp.concatenate([-st, st] * 2, axis=-1)
    return cos_a, sin_a, cos_b, sin_b


def _rotate(x, cos, sin_signed, half):
    n = x.shape[-1]
    lane = lax.broadcasted_iota(jnp.int32, x.shape, 1)
    first = (lane & (2 * half - 1)) < half
    partner = jnp.where(first, pltpu.roll(x, n - half, 1), pltpu.roll(x, half, 1))
    return x * cos + partner * sin_signed


def _prep_kernel(x_ref, w_ref, gq_ref, gk_ref, ca_ref, sa_ref, cb_ref, sb_ref,
                 gsum_ref, qa_ref, ka_ref, va_ref, qb_ref, kb_ref, vb_ref):
    xb = x_ref[0].astype(BF16)
    tm = xb.shape[0]
    q_scale = HEAD_DIM ** -0.5 * LOG2E
    ones_rows = (lax.broadcasted_iota(jnp.int32, (BF16_ROWS, tm), 0) == 0).astype(BF16)

    def proj(c0, width):
        return jnp.dot(xb, w_ref[:, c0:c0 + width], preferred_element_type=F32)

    def norm_rope_a(y, gain):
        ss = jnp.dot((y * y).astype(BF16), gsum_ref[...], preferred_element_type=F32)
        y = y * lax.rsqrt(ss * (1.0 / HEAD_DIM) + EPS) * gain
        return _rotate(y, ca_ref[...], sa_ref[...], HEAD_DIM // 4)

    def rope_b(y):
        return _rotate(y, cb_ref[...], sb_ref[...], HEAD_DIM // 2)

    def store_q(ref, y, c):
        yt = (y * q_scale).T.astype(ref.dtype)
        ref[0, 2 * c] = yt[:HEAD_DIM]
        ref[0, 2 * c + 1] = yt[HEAD_DIM:]

    def store_k(ref, y, c):
        y = y.astype(ref.dtype)
        ref[0, 2 * c] = y[:, :HEAD_DIM]
        ref[0, 2 * c + 1] = y[:, HEAD_DIM:]

    c0 = 0
    qa = proj(c0, COLS_A_Q)
    for c in range(COLS_A_Q // LANES):
        store_q(qa_ref, norm_rope_a(qa[:, c * LANES:(c + 1) * LANES], gq_ref[...]), c)
    c0 += COLS_A_Q
    store_k(ka_ref, norm_rope_a(proj(c0, COLS_A_KV), gk_ref[...]), 0)
    c0 += COLS_A_KV
    vat = proj(c0, COLS_A_KV).T.astype(va_ref.dtype)
    for h in range(A_KV_HEADS):
        va_ref[0, h, 0, :HEAD_DIM] = vat[h * HEAD_DIM:(h + 1) * HEAD_DIM]
        va_ref[0, h, 0, HEAD_DIM:] = ones_rows
    c0 += COLS_A_KV
    qb = proj(c0, COLS_B_QK)
    for c in range(COLS_B_QK // LANES):
        store_q(qb_ref, rope_b(qb[:, c * LANES:(c + 1) * LANES]), c)
    c0 += COLS_B_QK
    kb = proj(c0, COLS_B_QK)
    for c in range(COLS_B_QK // LANES):
        store_k(kb_ref, rope_b(kb[:, c * LANES:(c + 1) * LANES]), c)
    c0 += COLS_B_QK
    vb = proj(c0, COLS_B_V)
    for h in range(B_HEADS):
        vb_ref[0, h, 0, :B_V_DIM] = vb[:, h * B_V_DIM:(h + 1) * B_V_DIM].T.astype(vb_ref.dtype)
        vb_ref[0, h, 0, B_V_DIM:] = ones_rows


def _prep(x, w_qkv, gq, gk, tables, tm):
    bsz, seq, _ = x.shape
    nblk = seq // tm
    gsum = jnp.kron(jnp.eye(LANES // HEAD_DIM, dtype=F32), jnp.ones((HEAD_DIM, HEAD_DIM), F32)).astype(BF16)
    tok = lambda b, i: (i, 0)
    const = lambda b, i: (0, 0)
    qt_out = lambda nh: pl.BlockSpec((1, nh, HEAD_DIM, tm), lambda b, i: (b, 0, 0, i))
    k_out = lambda nh: pl.BlockSpec((1, nh, tm, HEAD_DIM), lambda b, i: (b, 0, i, 0))
    vt_out = lambda nh, r: pl.BlockSpec((1, nh, 1, r, tm), lambda b, i: (b, 0, i, 0, 0))
    rows_a = HEAD_DIM + BF16_ROWS
    rows_b = B_V_DIM + BF16_ROWS
    out_shape = (
        jax.ShapeDtypeStruct((bsz, A_Q_HEADS, HEAD_DIM, seq), BF16),
        jax.ShapeDtypeStruct((bsz, A_KV_HEADS, seq, HEAD_DIM), BF16),
        jax.ShapeDtypeStruct((bsz, A_KV_HEADS, nblk, rows_a, tm), BF16),
        jax.ShapeDtypeStruct((bsz, 2 * B_HEADS, HEAD_DIM, seq), BF16),
        jax.ShapeDtypeStruct((bsz, 2 * B_HEADS, seq, HEAD_DIM), BF16),
        jax.ShapeDtypeStruct((bsz, B_HEADS, nblk, rows_b, tm), BF16),
    )
    return pl.pallas_call(
        _prep_kernel,
        grid=(bsz, nblk),
        in_specs=[
            pl.BlockSpec((1, tm, D_MODEL), lambda b, i: (b, i, 0)),
            pl.BlockSpec((D_MODEL, COLS_QKV), const),
            pl.BlockSpec((1, LANES), const),
            pl.BlockSpec((1, LANES), const),
        ] + [pl.BlockSpec((tm, LANES), tok)] * 4 + [pl.BlockSpec((LANES, LANES), const)],
        out_specs=(qt_out(A_Q_HEADS), k_out(A_KV_HEADS), vt_out(A_KV_HEADS, rows_a),
                   qt_out(2 * B_HEADS), k_out(2 * B_HEADS), vt_out(B_HEADS, rows_b)),
        out_shape=out_shape,
        compiler_params=_cparams(("parallel", "parallel")),
        name="prep",
    )(x, w_qkv, gq, gk, *tables, gsum)


SCORE_ROW_PAD = 2 * LANES


def _flash_kernel(qt_ref, qn_ref, k_ref, vt_ref, o_ref, qcat_ref, qnext_ref, acc_ref, m_ref,
                  s0_ref, s1_ref, c0_ref, c1_ref, *, dv):
    group, _, tq = qt_ref.shape[1:]
    nchunk, _, tk = vt_ref.shape[2:]
    for r in range(group):
        qcat_ref[:, r * tq:(r + 1) * tq] = qt_ref[0, r]
        qnext_ref[:, r * tq:(r + 1) * tq] = qn_ref[0, r]
    acc_ref[...] = jnp.zeros_like(acc_ref)
    m_ref[...] = jnp.full_like(m_ref, -jnp.inf)

    def scores(q_ref, s_ref, c_ref, j):
        start = pl.multiple_of(j * tk, tk)
        k = k_ref[0, 0, pl.ds(start, tk), :]
        st = jnp.dot(k, q_ref[...], preferred_element_type=F32)
        s_ref[:, :st.shape[1]] = st
        c_ref[...] = jnp.max(st, axis=0, keepdims=True)

    def update(s_ref, c_ref, j):
        m_prev = m_ref[...]
        m_new = jnp.maximum(m_prev, c_ref[...])
        pt = jnp.exp2(s_ref[:, :m_new.shape[1]] - m_new).astype(BF16)
        alpha = jnp.exp2(m_prev - m_new)
        acc_ref[...] = alpha * acc_ref[...] + jnp.dot(vt_ref[0, 0, j], pt, preferred_element_type=F32)
        m_ref[...] = m_new

    @pl.when(pl.program_id(1) == 0)
    def _():
        scores(qcat_ref, s0_ref, c0_ref, 0)

    def body(i, carry):
        j = 2 * i
        scores(qcat_ref, s1_ref, c1_ref, j + 1)
        update(s0_ref, c0_ref, j)
        scores(qcat_ref, s0_ref, c0_ref, j + 2)
        update(s1_ref, c1_ref, j + 1)
        return carry

    lax.fori_loop(0, nchunk // 2 - 1, body, 0)
    scores(qcat_ref, s1_ref, c1_ref, nchunk - 1)
    update(s0_ref, c0_ref, nchunk - 2)
    scores(qnext_ref, s0_ref, c0_ref, 0)
    update(s1_ref, c1_ref, nchunk - 1)
    acc = acc_ref[...]
    ot = (acc[:dv] / acc[dv:dv + 1]).astype(o_ref.dtype)
    for r in range(group):
        o_ref[0, r] = ot[:, r * tq:(r + 1) * tq]


def _flash(qt, k, vt, *, v_of, tq, dv):
    n, group, _, seq = qt.shape
    nchunk, rows, tk = vt.shape[2:]
    nq = seq // tq
    q_spec = lambda nxt: pl.BlockSpec((1, group, HEAD_DIM, tq),
                                      lambda b, i: (b, 0, 0, jnp.minimum(i + nxt, nq - 1)))
    return pl.pallas_call(
        functools.partial(_flash_kernel, dv=dv),
        grid=(n, nq),
        in_specs=[
            q_spec(0), q_spec(1),
            pl.BlockSpec((1, 1, seq, HEAD_DIM), lambda b, i: (b, 0, 0, 0)),
            pl.BlockSpec((1, 1, nchunk, rows, tk), lambda b, i: (v_of(b), 0, 0, 0, 0)),
        ],
        out_specs=pl.BlockSpec((1, group, dv, tq), lambda b, i: (b, 0, 0, i)),
        out_shape=jax.ShapeDtypeStruct((n, group, dv, seq), BF16),
        scratch_shapes=[pltpu.VMEM((HEAD_DIM, group * tq), BF16),
                        pltpu.VMEM((HEAD_DIM, group * tq), BF16),
                        pltpu.VMEM((rows, group * tq), F32),
                        pltpu.VMEM((1, group * tq), F32),
                        pltpu.VMEM((tk, group * tq + SCORE_ROW_PAD), F32),
                        pltpu.VMEM((tk, group * tq + SCORE_ROW_PAD), F32),
                        pltpu.VMEM((1, group * tq), F32),
                        pltpu.VMEM((1, group * tq), F32)],
        compiler_params=_cparams(("parallel", "arbitrary")),
        name="flash_dv%d" % dv,
    )(qt, qt, k, vt)


def _layer_norm(x, g, b):
    mu = jnp.mean(x, axis=-1, keepdims=True)
    xc = x - mu
    var = jnp.mean(xc * xc, axis=-1, keepdims=True)
    return xc * lax.rsqrt(var + LN_EPS) * g + b


def _dot_t(at, b):
    return lax.dot_general(at, b, (((0,), (0,)), ((), ())), preferred_element_type=F32)


def _merge_kernel(x_ref, oa_ref, ob_ref, wg_ref, bg_ref, wpa_ref, wpb_ref, wo_ref, sub_ref,
                  lq1_ref, lk1_ref, lq2_ref, lk2_ref, g_ref, b_ref, h_ref, ht_ref):
    x = x_ref[0]
    xb = x.astype(BF16)
    tm = x.shape[0]
    lam = (jnp.exp(jnp.sum(lq1_ref[...] * lk1_ref[...], axis=-1, keepdims=True))
           - jnp.exp(jnp.sum(lq2_ref[...] * lk2_ref[...], axis=-1, keepdims=True)) + LAMBDA_INIT)
    gl = jnp.dot(xb, wg_ref[...], preferred_element_type=F32) + bg_ref[...]
    gate_a = jax.nn.sigmoid(gl[:, :D_MODEL])
    gate_b = jax.nn.sigmoid(gl[:, D_MODEL:])

    pa = _dot_t(oa_ref[0].reshape(COLS_A_Q, tm), wpa_ref[...])
    ybt = []
    for h in range(B_HEADS):
        y = ob_ref[0, 2 * h].astype(F32) - lam * ob_ref[0, 2 * h + 1].astype(F32)
        y = y * lax.rsqrt(jnp.mean(y * y, axis=0, keepdims=True) + EPS) * sub_ref[...]
        ybt.append((y * (1.0 - LAMBDA_INIT)).astype(BF16))
    pb = _dot_t(jnp.concatenate(ybt, axis=0), wpb_ref[...])
    merged = gate_a * pa + gate_b * pb
    pre = DN_ALPHA * x + jnp.dot(merged.astype(BF16), wo_ref[...], preferred_element_type=F32)
    h_out = _layer_norm(pre, g_ref[...], b_ref[...])
    h_ref[0] = h_out
    ht_ref[...] = h_out.T.astype(ht_ref.dtype)


def _merge(x, oat, obt, wg, bg, wpa, wpb, wo, sub, lams, g, b, tm):
    bsz, seq, _ = x.shape
    nblk = seq // tm
    const = lambda bb, i: (0, 0)
    vec = lambda w: pl.BlockSpec((1, w), const)
    return pl.pallas_call(
        _merge_kernel,
        grid=(bsz, nblk),
        in_specs=[
            pl.BlockSpec((1, tm, D_MODEL), lambda bb, i: (bb, i, 0)),
            pl.BlockSpec((1, A_Q_HEADS, HEAD_DIM, tm), lambda bb, i: (bb, 0, 0, i)),
            pl.BlockSpec((1, 2 * B_HEADS, B_V_DIM, tm), lambda bb, i: (bb, 0, 0, i)),
            pl.BlockSpec((D_MODEL, 2 * D_MODEL), const),
            vec(2 * D_MODEL),
            pl.BlockSpec((COLS_A_Q, D_MODEL), const),
            pl.BlockSpec((COLS_B_V, D_MODEL), const),
            pl.BlockSpec((D_MODEL, D_MODEL), const),
            pl.BlockSpec((B_V_DIM, 1), const),
            vec(HEAD_DIM), vec(HEAD_DIM), vec(HEAD_DIM), vec(HEAD_DIM),
            vec(D_MODEL), vec(D_MODEL),
        ],
        out_specs=(
            pl.BlockSpec((1, tm, D_MODEL), lambda bb, i: (bb, i, 0)),
            pl.BlockSpec((D_MODEL, tm), lambda bb, i: (0, bb * nblk + i)),
        ),
        out_shape=(
            jax.ShapeDtypeStruct((bsz, seq, D_MODEL), F32),
            jax.ShapeDtypeStruct((D_MODEL, bsz * seq), BF16),
        ),
        compiler_params=_cparams(("parallel", "parallel")),
        name="merge",
    )(x, oat, obt, wg, bg, wpa, wpb, wo, sub, *lams, g, b)


_NO_RANK = 32.0


SUBLANES = 8


def _sort_network(n):
    pairs, p = [], 1
    while p < n:
        k = p
        while k >= 1:
            for j in range(k % p, n - k, 2 * k):
                for i in range(min(k, n - j - k)):
                    if (i + j) // (2 * p) == (i + j + k) // (2 * p):
                        pairs.append((i + j, i + j + k))
            k //= 2
        p *= 2
    return pairs


def _top_values(s, n):
    vals = []
    if s.shape[0] != PEER_N_KEYS:
        for _ in range(n):
            m = jnp.max(s, axis=0, keepdims=True)
            vals.append(m)
            s = jnp.where(s == m, -jnp.inf, s)
        return vals
    rows = [s[r * SUBLANES:(r + 1) * SUBLANES] for r in range(s.shape[0] // SUBLANES)]
    for a, b in _sort_network(len(rows)):
        rows[a], rows[b] = jnp.maximum(rows[a], rows[b]), jnp.minimum(rows[a], rows[b])
    for i in range(n):
        m = jnp.max(rows[0], axis=0, keepdims=True)
        vals.append(m)
        hit = rows[0] == m
        depth = min(len(rows), n - i)
        below = rows[1:depth + 1]
        below = below + [jnp.full_like(rows[0], -jnp.inf)] * (depth - len(below))
        rows = [jnp.where(hit, below[k], rows[k]) for k in range(depth)]
    return vals


def _rank_codes(s, vals):
    rank = jnp.full(s.shape, _NO_RANK, F32)
    for i, v in enumerate(vals):
        rank = jnp.where(s == v, float(i), rank)
    return rank


def _route_kernel(ht_ref, wq_ref, keys_ref, cnt_ref, coef_ref, rank_ref, e2_ref):
    qt = jnp.dot(wq_ref[...], ht_ref[...], preferred_element_type=F32).astype(BF16)
    s1_all = jnp.dot(keys_ref[0, 0], qt[:PEER_HALF], preferred_element_type=F32)
    s2_all = jnp.dot(keys_ref[0, 1], qt[PEER_HALF:], preferred_element_type=F32)
    for g in range(s1_all.shape[1] // LANES):
        lanes = slice(g * LANES, (g + 1) * LANES)
        s1 = s1_all[:, lanes]
        s2 = s2_all[:, lanes]
        v1 = _top_values(s1, PEER_TOPK + 1)
        v2 = _top_values(s2, PEER_TOPK + 1)
        rank2 = _rank_codes(s2, v2)
        pair = {(k, l): v1[k] + v2[l] for k, l in _CAND_PAIRS}
        top = _top_values(jnp.concatenate([pair[p] for p in _CAND_PAIRS], axis=0), PEER_TOPK + 1)
        thr = 0.5 * (top[PEER_TOPK - 1] + top[PEER_TOPK])
        ev1 = [jnp.exp(v - v1[0]) for v in v1]
        ev2 = [jnp.exp(v - v2[0]) for v in v2]
        z = jnp.zeros_like(thr)
        pairs = [jnp.zeros_like(thr) for _ in v1]
        for k, l in _CAND_PAIRS:
            sel = pair[(k, l)] > thr
            z = z + jnp.where(sel, ev1[k] * ev2[l], 0.0)
            pairs[k] = pairs[k] + jnp.where(sel, 1.0, 0.0)
        cnt = jnp.zeros_like(s1)
        for k, v in enumerate(v1):
            cnt = jnp.where(s1 == v, pairs[k], cnt)
        cnt_ref[0, :, lanes] = cnt
        coef_ref[0, :, lanes] = jnp.exp(s1 - v1[0]) / z
        rank_ref[0, :, lanes] = rank2.astype(rank_ref.dtype)
        e2_ref[0, :, lanes] = jnp.exp(s2 - v2[0]).astype(e2_ref.dtype)


def _route(ht, wq_t, keys, tm):
    ntok = ht.shape[1]
    shape = (PEER_HEADS, PEER_N_KEYS, ntok)
    ospec = pl.BlockSpec((1, PEER_N_KEYS, tm), lambda t, h: (h, 0, t))
    return pl.pallas_call(
        _route_kernel,
        grid=(ntok // tm, PEER_HEADS),
        in_specs=[
            pl.BlockSpec((D_MODEL, tm), lambda t, h: (0, t)),
            pl.BlockSpec((2 * PEER_HALF, D_MODEL), lambda t, h: (h, 0)),
            pl.BlockSpec((1, 2, PEER_N_KEYS, PEER_HALF), lambda t, h: (h, 0, 0, 0)),
        ],
        out_specs=(ospec, ospec, ospec, ospec),
        out_shape=(jax.ShapeDtypeStruct(shape, F32), jax.ShapeDtypeStruct(shape, F32),
                   jax.ShapeDtypeStruct(shape, BF16), jax.ShapeDtypeStruct(shape, BF16)),
        compiler_params=_cparams(("parallel", "parallel")),
        name="route",
    )(ht, wq_t, keys)


def _gelu(a):
    return 0.5 * a * (1.0 + lax.erf(a * (2.0 ** -0.5)))


def _peer_kernel(ht_ref, u_ref, vt_ref, cnt_ref, coef_ref, rank_ref, e2_ref, h_ref, g_ref, b_ref,
                 o_ref, yt_ref, *, rows_per_step):
    e = pl.program_id(1)

    @pl.when(e == 0)
    def _():
        yt_ref[...] = jnp.zeros_like(yt_ref)

    at = jnp.dot(u_ref[...], ht_ref[...], preferred_element_type=F32)
    tm = at.shape[1]
    pieces = []
    for r in range(rows_per_step):
        gate = jnp.zeros((PEER_N_KEYS, tm), BF16)
        for h in range(PEER_HEADS):
            cnt = jnp.broadcast_to(cnt_ref[h, r:r + 1, :], (BF16_ROWS, tm)).astype(BF16)
            coef = jnp.broadcast_to(coef_ref[h, r:r + 1, :], (BF16_ROWS, tm)).astype(BF16)
            cnt = jnp.tile(cnt, (PEER_N_KEYS // BF16_ROWS, 1))
            coef = jnp.tile(coef, (PEER_N_KEYS // BF16_ROWS, 1))
            gate = gate + jnp.where(rank_ref[h] < cnt, e2_ref[h] * coef, jnp.zeros_like(coef))
        a = at[r * PEER_N_KEYS:(r + 1) * PEER_N_KEYS, :]
        pieces.append(_gelu(a.astype(BF16)) * gate)
    wt = jnp.concatenate(pieces, axis=0)
    yt_ref[...] += jnp.dot(vt_ref[...], wt, preferred_element_type=F32)

    @pl.when(e == pl.num_programs(1) - 1)
    def _():
        pre = DN_ALPHA * h_ref[...] + yt_ref[...].T
        o_ref[...] = _layer_norm(pre, g_ref[...], b_ref[...])


def _peer(ht, u, vt, cnt, coef, rank, e2, h, g, b, tm, eb):
    ntok = ht.shape[1]
    rows = eb // PEER_N_KEYS
    full = pl.BlockSpec((PEER_HEADS, PEER_N_KEYS, tm), lambda t, e: (0, 0, t))
    part = pl.BlockSpec((PEER_HEADS, rows, tm), lambda t, e: (0, e, t))
    vec = pl.BlockSpec((1, D_MODEL), lambda t, e: (0, 0))
    return pl.pallas_call(
        functools.partial(_peer_kernel, rows_per_step=rows),
        grid=(ntok // tm, PEER_N_EXPERTS // eb),
        in_specs=[
            pl.BlockSpec((D_MODEL, tm), lambda t, e: (0, t)),
            pl.BlockSpec((eb, D_MODEL), lambda t, e: (e, 0)),
            pl.BlockSpec((D_MODEL, eb), lambda t, e: (0, e)),
            part, part, full, full,
            pl.BlockSpec((tm, D_MODEL), lambda t, e: (t, 0)),
            vec, vec,
        ],
        out_specs=pl.BlockSpec((tm, D_MODEL), lambda t, e: (t, 0)),
        out_shape=jax.ShapeDtypeStruct((ntok, D_MODEL), F32),
        scratch_shapes=[pltpu.VMEM((D_MODEL, tm), F32)],
        compiler_params=_cparams(("parallel", "arbitrary")),
        name="peer",
    )(ht, u, vt, cnt, coef, rank, e2, h, g, b)


def _hybrid_block(x, w_in, b_gate, q_norm_a, k_norm_a, lambda_q1, lambda_k1, lambda_q2, lambda_k2,
                  subln_b, w_proj_a, w_proj_b, w_out, ln1_g, ln1_b, w_query, sub_keys,
                  u_table, v_table, ln2_g, ln2_b, *, tk, tq_a, tq_b, tm_merge, tm_route, tm_peer, eb):
    bsz, seq, _ = x.shape
    row = lambda a: a.reshape(1, -1).astype(F32)
    two = lambda a: jnp.concatenate([row(a), row(a)], axis=-1)

    qat, ka, vat, qbt, kb, vbt = _prep(x, w_in[:, :COLS_QKV].astype(BF16), two(q_norm_a), two(k_norm_a),
                                       _rope_tables(seq), tk)

    nchunk = seq // tk
    oat = _flash(qat.reshape(bsz * A_KV_HEADS, A_GROUP, HEAD_DIM, seq),
                 ka.reshape(bsz * A_KV_HEADS, 1, seq, HEAD_DIM),
                 vat.reshape(bsz * A_KV_HEADS, 1, nchunk, HEAD_DIM + BF16_ROWS, tk),
                 v_of=lambda n: n, tq=tq_a, dv=HEAD_DIM)
    obt = _flash(qbt.reshape(bsz * 2 * B_HEADS, 1, HEAD_DIM, seq),
                 kb.reshape(bsz * 2 * B_HEADS, 1, seq, HEAD_DIM),
                 vbt.reshape(bsz * B_HEADS, 1, nchunk, B_V_DIM + BF16_ROWS, tk),
                 v_of=lambda n: n // 2, tq=tq_b, dv=B_V_DIM)

    h, ht = _merge(x, oat.reshape(bsz, A_Q_HEADS, HEAD_DIM, seq), obt.reshape(bsz, 2 * B_HEADS, B_V_DIM, seq),
                   w_in[:, COLS_QKV:].astype(BF16), row(b_gate), w_proj_a.astype(BF16),
                   w_proj_b.astype(BF16), w_out.astype(BF16), subln_b.reshape(-1, 1).astype(F32),
                   (row(lambda_q1), row(lambda_k1), row(lambda_q2), row(lambda_k2)),
                   row(ln1_g), row(ln1_b), tm_merge)

    cnt, coef, rank, e2 = _route(ht, w_query.T.astype(BF16), sub_keys.astype(BF16), tm_route)
    out = _peer(ht, u_table.astype(BF16), v_table.T.astype(BF16), cnt, coef, rank, e2,
                h.reshape(bsz * seq, D_MODEL), row(ln2_g), row(ln2_b), tm_peer, eb)
    return out.reshape(bsz, seq, D_MODEL)


def kernel(x, w_in, b_gate, q_norm_a, k_norm_a, lambda_q1, lambda_k1, lambda_q2, lambda_k2, subln_b,
           w_proj_a, w_proj_b, w_out, ln1_g, ln1_b, w_query, sub_keys, u_table, v_table, ln2_g, ln2_b):
    return _hybrid_block(
        x, w_in[0], b_gate[0], q_norm_a[0], k_norm_a[0], lambda_q1[0], lambda_k1[0], lambda_q2[0],
        lambda_k2[0], subln_b[0], w_proj_a[0], w_proj_b[0], w_out[0], ln1_g[0], ln1_b[0], w_query[0],
        sub_keys[0], u_table[0], v_table[0], ln2_g[0], ln2_b[0],
        tk=512, tq_a=256, tq_b=1024, tm_merge=512, tm_route=512, tm_peer=512, eb=2048)
```

```python
import functools
import math

import jax
import jax.numpy as jnp
from jax import lax
from jax.experimental import pallas as pl
from jax.experimental.pallas import tpu as pltpu

F32 = jnp.float32
BF16 = jnp.bfloat16

D_MODEL = 1024
HEAD_DIM = 64
A_Q_HEADS = 8
A_KV_HEADS = 2
A_GROUP = A_Q_HEADS // A_KV_HEADS
B_HEADS = 4
B_V_DIM = 2 * HEAD_DIM
GRID_W = 64
ROPE_THETA = 10000.0
EPS = 1e-6
LN_EPS = 1e-5
PEER_HEADS = 8
PEER_N_KEYS = 128
PEER_N_EXPERTS = PEER_N_KEYS * PEER_N_KEYS
PEER_HALF = 128
PEER_TOPK = 16
DEPTH = 1
DN_ALPHA = (2.0 * DEPTH) ** 0.25
LAMBDA_INIT = 0.8 - 0.6 * math.exp(-0.3 * 0)

COLS_A_Q = A_Q_HEADS * HEAD_DIM
COLS_A_KV = A_KV_HEADS * HEAD_DIM
COLS_B_QK = B_HEADS * 2 * HEAD_DIM
COLS_B_V = B_HEADS * B_V_DIM
COLS_QKV = COLS_A_Q + 2 * COLS_A_KV + 2 * COLS_B_QK + COLS_B_V

LANES = 128
BF16_ROWS = 16
VMEM_LIMIT = 48 * 1024 * 1024
LOG2E = math.log2(math.e)

_CAND_PAIRS = tuple((k, l) for k in range(PEER_TOPK + 1) for l in range(PEER_TOPK + 1)
                    if (k + 1) * (l + 1) <= PEER_TOPK + 1)


def _cparams(sem):
    return pltpu.CompilerParams(dimension_semantics=sem, vmem_limit_bytes=VMEM_LIMIT)


def _rope_tables(seq):
    t = jnp.arange(seq, dtype=jnp.int32)

    def tab(pos, dim):
        freqs = ROPE_THETA ** (-jnp.arange(0, dim, 2, dtype=F32) / dim)
        ang = pos.astype(F32)[:, None] * freqs[None, :]
        return jnp.cos(ang), jnp.sin(ang)

    cr, sr = tab(t // GRID_W, HEAD_DIM // 2)
    cc, sc = tab(t % GRID_W, HEAD_DIM // 2)
    ct, st = tab(t, HEAD_DIM)
    cos_a = jnp.concatenate([cr, cr, cc, cc] * 2, axis=-1)
    sin_a = jnp.concatenate([-sr, sr, -sc, sc] * 2, axis=-1)
    cos_b = jnp.concatenate([ct, ct] * 2, axis=-1)
    sin_b = jnp.concatenate([-st, st] * 2, axis=-1)
    return cos_a, sin_a, cos_b, sin_b


def _rotate(x, cos, sin_signed, half):
    n = x.shape[-1]
    lane = lax.broadcasted_iota(jnp.int32, x.shape, 1)
    first = (lane & (2 * half - 1)) < half
    partner = jnp.where(first, pltpu.roll(x, n - half, 1), pltpu.roll(x, half, 1))
    return x * cos + partner * sin_signed


def _prep_kernel(x_ref, w_ref, gq_ref, gk_ref, ca_ref, sa_ref, cb_ref, sb_ref,
                 gsum_ref, qa_ref, ka_ref, va_ref, qb_ref, kb_ref, vb_ref):
    xb = x_ref[0].astype(BF16)
    tm = xb.shape[0]
    q_scale = HEAD_DIM ** -0.5 * LOG2E
    ones_rows = (lax.broadcasted_iota(jnp.int32, (BF16_ROWS, tm), 0) == 0).astype(BF16)

    def proj(c0, width):
        return jnp.dot(xb, w_ref[:, c0:c0 + width], preferred_element_type=F32)

    def norm_rope_a(y, gain):
        ss = jnp.dot((y * y).astype(BF16), gsum_ref[...], preferred_element_type=F32)
        y = y * lax.rsqrt(ss * (1.0 / HEAD_DIM) + EPS) * gain
        return _rotate(y, ca_ref[...], sa_ref[...], HEAD_DIM // 4)

    def rope_b(y):
        return _rotate(y, cb_ref[...], sb_ref[...], HEAD_DIM // 2)

    def store_q(ref, y, c):
        yt = (y * q_scale).T.astype(ref.dtype)
        ref[0, 2 * c] = yt[:HEAD_DIM]
        ref[0, 2 * c + 1] = yt[HEAD_DIM:]

    def store_k(ref, y, c):
        y = y.astype(ref.dtype)
        ref[0, 2 * c] = y[:, :HEAD_DIM]
        ref[0, 2 * c + 1] = y[:, HEAD_DIM:]

    c0 = 0
    qa = proj(c0, COLS_A_Q)
    for c in range(COLS_A_Q // LANES):
        store_q(qa_ref, norm_rope_a(qa[:, c * LANES:(c + 1) * LANES], gq_ref[...]), c)
    c0 += COLS_A_Q
    store_k(ka_ref, norm_rope_a(proj(c0, COLS_A_KV), gk_ref[...]), 0)
    c0 += COLS_A_KV
    vat = proj(c0, COLS_A_KV).T.astype(va_ref.dtype)
    for h in range(A_KV_HEADS):
        va_ref[0, h, 0, :HEAD_DIM] = vat[h * HEAD_DIM:(h + 1) * HEAD_DIM]
        va_ref[0, h, 0, HEAD_DIM:] = ones_rows
    c0 += COLS_A_KV
    qb = proj(c0, COLS_B_QK)
    for c in range(COLS_B_QK // LANES):
        store_q(qb_ref, rope_b(qb[:, c * LANES:(c + 1) * LANES]), c)
    c0 += COLS_B_QK
    kb = proj(c0, COLS_B_QK)
    for c in range(COLS_B_QK // LANES):
        store_k(kb_ref, rope_b(kb[:, c * LANES:(c + 1) * LANES]), c)
    c0 += COLS_B_QK
    vb = proj(c0, COLS_B_V)
    for h in range(B_HEADS):
        vb_ref[0, h, 0, :B_V_DIM] = vb[:, h * B_V_DIM:(h + 1) * B_V_DIM].T.astype(vb_ref.dtype)
        vb_ref[0, h, 0, B_V_DIM:] = ones_rows


def _prep(x, w_qkv, gq, gk, tables, tm):
    bsz, seq, _ = x.shape
    nblk = seq // tm
    gsum = jnp.kron(jnp.eye(LANES // HEAD_DIM, dtype=F32), jnp.ones((HEAD_DIM, HEAD_DIM), F32)).astype(BF16)
    tok = lambda b, i: (i, 0)
    const = lambda b, i: (0, 0)
    qt_out = lambda nh: pl.BlockSpec((1, nh, HEAD_DIM, tm), lambda b, i: (b, 0, 0, i))
    k_out = lambda nh: pl.BlockSpec((1, nh, tm, HEAD_DIM), lambda b, i: (b, 0, i, 0))
    vt_out = lambda nh, r: pl.BlockSpec((1, nh, 1, r, tm), lambda b, i: (b, 0, i, 0, 0))
    rows_a = HEAD_DIM + BF16_ROWS
    rows_b = B_V_DIM + BF16_ROWS
    out_shape = (
        jax.ShapeDtypeStruct((bsz, A_Q_HEADS, HEAD_DIM, seq), BF16),
        jax.ShapeDtypeStruct((bsz, A_KV_HEADS, seq, HEAD_DIM), BF16),
        jax.ShapeDtypeStruct((bsz, A_KV_HEADS, nblk, rows_a, tm), BF16),
        jax.ShapeDtypeStruct((bsz, 2 * B_HEADS, HEAD_DIM, seq), BF16),
        jax.ShapeDtypeStruct((bsz, 2 * B_HEADS, seq, HEAD_DIM), BF16),
        jax.ShapeDtypeStruct((bsz, B_HEADS, nblk, rows_b, tm), BF16),
    )
    return pl.pallas_call(
        _prep_kernel,
        grid=(bsz, nblk),
        in_specs=[
            pl.BlockSpec((1, tm, D_MODEL), lambda b, i: (b, i, 0)),
            pl.BlockSpec((D_MODEL, COLS_QKV), const),
            pl.BlockSpec((1, LANES), const),
            pl.BlockSpec((1, LANES), const),
        ] + [pl.BlockSpec((tm, LANES), tok)] * 4 + [pl.BlockSpec((LANES, LANES), const)],
        out_specs=(qt_out(A_Q_HEADS), k_out(A_KV_HEADS), vt_out(A_KV_HEADS, rows_a),
                   qt_out(2 * B_HEADS), k_out(2 * B_HEADS), vt_out(B_HEADS, rows_b)),
        out_shape=out_shape,
        compiler_params=_cparams(("parallel", "parallel")),
        name="prep",
    )(x, w_qkv, gq, gk, *tables, gsum)


SCORE_ROW_PAD = 2 * LANES


def _flash_kernel(qt_ref, qn_ref, k_ref, vt_ref, o_ref, qcat_ref, qnext_ref, acc_ref, m_ref,
                  s0_ref, s1_ref, c0_ref, c1_ref, *, dv):
    group, _, tq = qt_ref.shape[1:]
    nchunk, _, tk = vt_ref.shape[2:]
    for r in range(group):
        qcat_ref[:, r * tq:(r + 1) * tq] = qt_ref[0, r]
        qnext_ref[:, r * tq:(r + 1) * tq] = qn_ref[0, r]
    acc_ref[...] = jnp.zeros_like(acc_ref)
    m_ref[...] = jnp.full_like(m_ref, -jnp.inf)

    def scores(q_ref, s_ref, c_ref, j):
        start = pl.multiple_of(j * tk, tk)
        k = k_ref[0, 0, pl.ds(start, tk), :]
        st = jnp.dot(k, q_ref[...], preferred_element_type=F32)
        s_ref[:, :st.shape[1]] = st
        c_ref[...] = jnp.max(st, axis=0, keepdims=True)

    def update(s_ref, c_ref, j):
        m_prev = m_ref[...]
        m_new = jnp.maximum(m_prev, c_ref[...])
        pt = jnp.exp2(s_ref[:, :m_new.shape[1]] - m_new).astype(BF16)
        alpha = jnp.exp2(m_prev - m_new)
        acc_ref[...] = alpha * acc_ref[...] + jnp.dot(vt_ref[0, 0, j], pt, preferred_element_type=F32)
        m_ref[...] = m_new

    @pl.when(pl.program_id(1) == 0)
    def _():
        scores(qcat_ref, s0_ref, c0_ref, 0)

    def body(i, carry):
        j = 2 * i
        scores(qcat_ref, s1_ref, c1_ref, j + 1)
        update(s0_ref, c0_ref, j)
        scores(qcat_ref, s0_ref, c0_ref, j + 2)
        update(s1_ref, c1_ref, j + 1)
        return carry

    lax.fori_loop(0, nchunk // 2 - 1, body, 0)
    scores(qcat_ref, s1_ref, c1_ref, nchunk - 1)
    update(s0_ref, c0_ref, nchunk - 2)
    scores(qnext_ref, s0_ref, c0_ref, 0)
    update(s1_ref, c1_ref, nchunk - 1)
    acc = acc_ref[...]
    ot = (acc[:dv] / acc[dv:dv + 1]).astype(o_ref.dtype)
    for r in range(group):
        o_ref[0, r] = ot[:, r * tq:(r + 1) * tq]


def _flash(qt, k, vt, *, v_of, tq, dv):
    n, group, _, seq = qt.shape
    nchunk, rows, tk = vt.shape[2:]
    nq = seq // tq
    q_spec = lambda nxt: pl.BlockSpec((1, group, HEAD_DIM, tq),
                                      lambda b, i: (b, 0, 0, jnp.minimum(i + nxt, nq - 1)))
    return pl.pallas_call(
        functools.partial(_flash_kernel, dv=dv),
        grid=(n, nq),
        in_specs=[
            q_spec(0), q_spec(1),
            pl.BlockSpec((1, 1, seq, HEAD_DIM), lambda b, i: (b, 0, 0, 0)),
            pl.BlockSpec((1, 1, nchunk, rows, tk), lambda b, i: (v_of(b), 0, 0, 0, 0)),
        ],
        out_specs=pl.BlockSpec((1, group, dv, tq), lambda b, i: (b, 0, 0, i)),
        out_shape=jax.ShapeDtypeStruct((n, group, dv, seq), BF16),
        scratch_shapes=[pltpu.VMEM((HEAD_DIM, group * tq), BF16),
                        pltpu.VMEM((HEAD_DIM, group * tq), BF16),
                        pltpu.VMEM((rows, group * tq), F32),
                        pltpu.VMEM((1, group * tq), F32),
                        pltpu.VMEM((tk, group * tq + SCORE_ROW_PAD), F32),
                        pltpu.VMEM((tk, group * tq + SCORE_ROW_PAD), F32),
                        pltpu.VMEM((1, group * tq), F32),
                        pltpu.VMEM((1, group * tq), F32)],
        compiler_params=_cparams(("parallel", "arbitrary")),
        name="flash_dv%d" % dv,
    )(qt, qt, k, vt)


def _layer_norm(x, g, b):
    mu = jnp.mean(x, axis=-1, keepdims=True)
    xc = x - mu
    var = jnp.mean(xc * xc, axis=-1, keepdims=True)
    return xc * lax.rsqrt(var + LN_EPS) * g + b


def _dot_t(at, b):
    return lax.dot_general(at, b, (((0,), (0,)), ((), ())), preferred_element_type=F32)


def _merge_kernel(x_ref, oa_ref, ob_ref, wg_ref, bg_ref, wpa_ref, wpb_ref, wo_ref, sub_ref,
                  lq1_ref, lk1_ref, lq2_ref, lk2_ref, g_ref, b_ref, h_ref, ht_ref):
    x = x_ref[0]
    xb = x.astype(BF16)
    tm = x.shape[0]
    lam = (jnp.exp(jnp.sum(lq1_ref[...] * lk1_ref[...], axis=-1, keepdims=True))
           - jnp.exp(jnp.sum(lq2_ref[...] * lk2_ref[...], axis=-1, keepdims=True)) + LAMBDA_INIT)
    gl = jnp.dot(xb, wg_ref[...], preferred_element_type=F32) + bg_ref[...]
    gate_a = jax.nn.sigmoid(gl[:, :D_MODEL])
    gate_b = jax.nn.sigmoid(gl[:, D_MODEL:])

    pa = _dot_t(oa_ref[0].reshape(COLS_A_Q, tm), wpa_ref[...])
    ybt = []
    for h in range(B_HEADS):
        y = ob_ref[0, 2 * h].astype(F32) - lam * ob_ref[0, 2 * h + 1].astype(F32)
        y = y * lax.rsqrt(jnp.mean(y * y, axis=0, keepdims=True) + EPS) * sub_ref[...]
        ybt.append((y * (1.0 - LAMBDA_INIT)).astype(BF16))
    pb = _dot_t(jnp.concatenate(ybt, axis=0), wpb_ref[...])
    merged = gate_a * pa + gate_b * pb
    pre = DN_ALPHA * x + jnp.dot(merged.astype(BF16), wo_ref[...], preferred_element_type=F32)
    h_out = _layer_norm(pre, g_ref[...], b_ref[...])
    h_ref[0] = h_out
    ht_ref[...] = h_out.T.astype(ht_ref.dtype)


def _merge(x, oat, obt, wg, bg, wpa, wpb, wo, sub, lams, g, b, tm):
    bsz, seq, _ = x.shape
    nblk = seq // tm
    const = lambda bb, i: (0, 0)
    vec = lambda w: pl.BlockSpec((1, w), const)
    return pl.pallas_call(
        _merge_kernel,
        grid=(bsz, nblk),
        in_specs=[
            pl.BlockSpec((1, tm, D_MODEL), lambda bb, i: (bb, i, 0)),
            pl.BlockSpec((1, A_Q_HEADS, HEAD_DIM, tm), lambda bb, i: (bb, 0, 0, i)),
            pl.BlockSpec((1, 2 * B_HEADS, B_V_DIM, tm), lambda bb, i: (bb, 0, 0, i)),
            pl.BlockSpec((D_MODEL, 2 * D_MODEL), const),
            vec(2 * D_MODEL),
            pl.BlockSpec((COLS_A_Q, D_MODEL), const),
            pl.BlockSpec((COLS_B_V, D_MODEL), const),
            pl.BlockSpec((D_MODEL, D_MODEL), const),
            pl.BlockSpec((B_V_DIM, 1), const),
            vec(HEAD_DIM), vec(HEAD_DIM), vec(HEAD_DIM), vec(HEAD_DIM),
            vec(D_MODEL), vec(D_MODEL),
        ],
        out_specs=(
            pl.BlockSpec((1, tm, D_MODEL), lambda bb, i: (bb, i, 0)),
            pl.BlockSpec((D_MODEL, tm), lambda bb, i: (0, bb * nblk + i)),
        ),
        out_shape=(
            jax.ShapeDtypeStruct((bsz, seq, D_MODEL), F32),
            jax.ShapeDtypeStruct((D_MODEL, bsz * seq), BF16),
        ),
        compiler_params=_cparams(("parallel", "parallel")),
        name="merge",
    )(x, oat, obt, wg, bg, wpa, wpb, wo, sub, *lams, g, b)


_NO_RANK = 32.0


SUBLANES = 8


def _sort_network(n):
    pairs, p = [], 1
    while p < n:
        k = p
        while k >= 1:
            for j in range(k % p, n - k, 2 * k):
                for i in range(min(k, n - j - k)):
                    if (i + j) // (2 * p) == (i + j + k) // (2 * p):
                        pairs.append((i + j, i + j + k))
            k //= 2
        p *= 2
    return pairs


def _top_values(s, n):
    vals = []
    if s.shape[0] != PEER_N_KEYS:
        for _ in range(n):
            m = jnp.max(s, axis=0, keepdims=True)
            vals.append(m)
            s = jnp.where(s == m, -jnp.inf, s)
        return vals
    rows = [s[r * SUBLANES:(r + 1) * SUBLANES] for r in range(s.shape[0] // SUBLANES)]
    for a, b in _sort_network(len(rows)):
        rows[a], rows[b] = jnp.maximum(rows[a], rows[b]), jnp.minimum(rows[a], rows[b])
    for i in range(n):
        m = jnp.max(rows[0], axis=0, keepdims=True)
        vals.append(m)
        hit = rows[0] == m
        depth = min(len(rows), n - i)
        below = rows[1:depth + 1]
        below = below + [jnp.full_like(rows[0], -jnp.inf)] * (depth - len(below))
        rows = [jnp.where(hit, below[k], rows[k]) for k in range(depth)]
    return vals


def _rank_codes(s, vals):
    rank = jnp.full(s.shape, _NO_RANK, F32)
    for i, v in enumerate(vals):
        rank = jnp.where(s == v, float(i), rank)
    return rank


def _route_kernel(ht_ref, wq_ref, keys_ref, cnt_ref, coef_ref, rank_ref, e2_ref):
    qt = jnp.dot(wq_ref[...], ht_ref[...], preferred_element_type=F32).astype(BF16)
    s1_all = jnp.dot(keys_ref[0, 0], qt[:PEER_HALF], preferred_element_type=F32)
    s2_all = jnp.dot(keys_ref[0, 1], qt[PEER_HALF:], preferred_element_type=F32)
    for g in range(s1_all.shape[1] // LANES):
        lanes = slice(g * LANES, (g + 1) * LANES)
        s1 = s1_all[:, lanes]
        s2 = s2_all[:, lanes]
        v1 = _top_values(s1, PEER_TOPK + 1)
        v2 = _top_values(s2, PEER_TOPK + 1)
        rank2 = _rank_codes(s2, v2[:PEER_TOPK])
        pair = {(k, l): v1[k] + v2[l] for k, l in _CAND_PAIRS}
        top = _top_values(jnp.concatenate([pair[p] for p in _CAND_PAIRS], axis=0), PEER_TOPK + 1)
        thr = 0.5 * (top[PEER_TOPK - 1] + top[PEER_TOPK])
        ev1 = [jnp.exp(v - v1[0]) for v in v1]
        ev2 = [jnp.exp(v - v2[0]) for v in v2]
        z = jnp.zeros_like(thr)
        pairs = [jnp.zeros_like(thr) for _ in v1]
        for k, l in _CAND_PAIRS:
            sel = pair[(k, l)] > thr
            z = z + jnp.where(sel, ev1[k] * ev2[l], 0.0)
            pairs[k] = pairs[k] + jnp.where(sel, 1.0, 0.0)
        cnt = jnp.zeros_like(s1)
        for k, v in enumerate(v1[:PEER_TOPK]):
            cnt = jnp.where(s1 == v, pairs[k], cnt)
        cnt_ref[0, :, lanes] = cnt
        coef_ref[0, :, lanes] = jnp.exp(s1 - v1[0]) / z
        rank_ref[0, :, lanes] = rank2.astype(rank_ref.dtype)
        e2_ref[0, :, lanes] = jnp.exp(s2 - v2[0]).astype(e2_ref.dtype)


def _route(ht, wq_t, keys, tm):
    ntok = ht.shape[1]
    shape = (PEER_HEADS, PEER_N_KEYS, ntok)
    ospec = pl.BlockSpec((1, PEER_N_KEYS, tm), lambda t, h: (h, 0, t))
    return pl.pallas_call(
        _route_kernel,
        grid=(ntok // tm, PEER_HEADS),
        in_specs=[
            pl.BlockSpec((D_MODEL, tm), lambda t, h: (0, t)),
            pl.BlockSpec((2 * PEER_HALF, D_MODEL), lambda t, h: (h, 0)),
            pl.BlockSpec((1, 2, PEER_N_KEYS, PEER_HALF), lambda t, h: (h, 0, 0, 0)),
        ],
        out_specs=(ospec, ospec, ospec, ospec),
        out_shape=(jax.ShapeDtypeStruct(shape, F32), jax.ShapeDtypeStruct(shape, F32),
                   jax.ShapeDtypeStruct(shape, BF16), jax.ShapeDtypeStruct(shape, BF16)),
        compiler_params=_cparams(("parallel", "parallel")),
        name="route",
    )(ht, wq_t, keys)


def _gelu(a):
    return 0.5 * a * (1.0 + lax.erf(a * (2.0 ** -0.5)))


def _peer_kernel(ht_ref, u_ref, vt_ref, cnt_ref, coef_ref, rank_ref, e2_ref, h_ref, g_ref, b_ref,
                 o_ref, yt_ref, *, rows_per_step):
    e = pl.program_id(1)

    @pl.when(e == 0)
    def _():
        yt_ref[...] = jnp.zeros_like(yt_ref)

    at = jnp.dot(u_ref[...], ht_ref[...], preferred_element_type=F32)
    tm = at.shape[1]
    pieces = []
    for r in range(rows_per_step):
        gate = jnp.zeros((PEER_N_KEYS, tm), BF16)
        for h in range(PEER_HEADS):
            cnt = jnp.broadcast_to(cnt_ref[h, r:r + 1, :], (BF16_ROWS, tm)).astype(BF16)
            coef = jnp.broadcast_to(coef_ref[h, r:r + 1, :], (BF16_ROWS, tm)).astype(BF16)
            cnt = jnp.tile(cnt, (PEER_N_KEYS // BF16_ROWS, 1))
            coef = jnp.tile(coef, (PEER_N_KEYS // BF16_ROWS, 1))
            gate = gate + jnp.where(rank_ref[h] < cnt, e2_ref[h] * coef, jnp.zeros_like(coef))
        a = at[r * PEER_N_KEYS:(r + 1) * PEER_N_KEYS, :]
        pieces.append(_gelu(a.astype(BF16)) * gate)
    wt = jnp.concatenate(pieces, axis=0)
    yt_ref[...] += jnp.dot(vt_ref[...], wt, preferred_element_type=F32)

    @pl.when(e == pl.num_programs(1) - 1)
    def _():
        pre = DN_ALPHA * h_ref[...] + yt_ref[...].T
        o_ref[...] = _layer_norm(pre, g_ref[...], b_ref[...])


def _peer(ht, u, vt, cnt, coef, rank, e2, h, g, b, tm, eb):
    ntok = ht.shape[1]
    rows = eb // PEER_N_KEYS
    full = pl.BlockSpec((PEER_HEADS, PEER_N_KEYS, tm), lambda t, e: (0, 0, t))
    part = pl.BlockSpec((PEER_HEADS, rows, tm), lambda t, e: (0, e, t))
    vec = pl.BlockSpec((1, D_MODEL), lambda t, e: (0, 0))
    return pl.pallas_call(
        functools.partial(_peer_kernel, rows_per_step=rows),
        grid=(ntok // tm, PEER_N_EXPERTS // eb),
        in_specs=[
            pl.BlockSpec((D_MODEL, tm), lambda t, e: (0, t)),
            pl.BlockSpec((eb, D_MODEL), lambda t, e: (e, 0)),
            pl.BlockSpec((D_MODEL, eb), lambda t, e: (0, e)),
            part, part, full, full,
            pl.BlockSpec((tm, D_MODEL), lambda t, e: (t, 0)),
            vec, vec,
        ],
        out_specs=pl.BlockSpec((tm, D_MODEL), lambda t, e: (t, 0)),
        out_shape=jax.ShapeDtypeStruct((ntok, D_MODEL), F32),
        scratch_shapes=[pltpu.VMEM((D_MODEL, tm), F32)],
        compiler_params=_cparams(("parallel", "arbitrary")),
        name="peer",
    )(ht, u, vt, cnt, coef, rank, e2, h, g, b)


def _hybrid_block(x, w_in, b_gate, q_norm_a, k_norm_a, lambda_q1, lambda_k1, lambda_q2, lambda_k2,
                  subln_b, w_proj_a, w_proj_b, w_out, ln1_g, ln1_b, w_query, sub_keys,
                  u_table, v_table, ln2_g, ln2_b, *, tk, tq_a, tq_b, tm_merge, tm_route, tm_peer, eb):
    bsz, seq, _ = x.shape
    row = lambda a: a.reshape(1, -1).astype(F32)
    two = lambda a: jnp.concatenate([row(a), row(a)], axis=-1)

    qat, ka, vat, qbt, kb, vbt = _prep(x, w_in[:, :COLS_QKV].astype(BF16), two(q_norm_a), two(k_norm_a),
                                       _rope_tables(seq), tk)

    nchunk = seq // tk
    oat = _flash(qat.reshape(bsz * A_KV_HEADS, A_GROUP, HEAD_DIM, seq),
                 ka.reshape(bsz * A_KV_HEADS, 1, seq, HEAD_DIM),
                 vat.reshape(bsz * A_KV_HEADS, 1, nchunk, HEAD_DIM + BF16_ROWS, tk),
                 v_of=lambda n: n, tq=tq_a, dv=HEAD_DIM)
    obt = _flash(qbt.reshape(bsz * 2 * B_HEADS, 1, HEAD_DIM, seq),
                 kb.reshape(bsz * 2 * B_HEADS, 1, seq, HEAD_DIM),
                 vbt.reshape(bsz * B_HEADS, 1, nchunk, B_V_DIM + BF16_ROWS, tk),
                 v_of=lambda n: n // 2, tq=tq_b, dv=B_V_DIM)

    h, ht = _merge(x, oat.reshape(bsz, A_Q_HEADS, HEAD_DIM, seq), obt.reshape(bsz, 2 * B_HEADS, B_V_DIM, seq),
                   w_in[:, COLS_QKV:].astype(BF16), row(b_gate), w_proj_a.astype(BF16),
                   w_proj_b.astype(BF16), w_out.astype(BF16), subln_b.reshape(-1, 1).astype(F32),
                   (row(lambda_q1), row(lambda_k1), row(lambda_q2), row(lambda_k2)),
                   row(ln1_g), row(ln1_b), tm_merge)

    cnt, coef, rank, e2 = _route(ht, w_query.T.astype(BF16), sub_keys.astype(BF16), tm_route)
    out = _peer(ht, u_table.astype(BF16), v_table.T.astype(BF16), cnt, coef, rank, e2,
                h.reshape(bsz * seq, D_MODEL), row(ln2_g), row(ln2_b), tm_peer, eb)
    return out.reshape(bsz, seq, D_MODEL)


def kernel(x, w_in, b_gate, q_norm_a, k_norm_a, lambda_q1, lambda_k1, lambda_q2, lambda_k2, subln_b,
           w_proj_a, w_proj_b, w_out, ln1_g, ln1_b, w_query, sub_keys, u_table, v_table, ln2_g, ln2_b):
    return _hybrid_block(
        x, w_in[0], b_gate[0], q_norm_a[0], k_norm_a[0], lambda_q1[0], lambda_k1[0], lambda_q2[0],
        lambda_k2[0], subln_b[0], w_proj_a[0], w_proj_b[0], w_out[0], ln1_g[0], ln1_b[0], w_query[0],
        sub_keys[0], u_table[0], v_table[0], ln2_g[0], ln2_b[0],
        tk=512, tq_a=256, tq_b=1024, tm_merge=512, tm_route=1024, tm_peer=512, eb=2048)
```

```python
import functools
import math

import jax
import jax.numpy as jnp
import numpy as np
from jax import lax
from jax.experimental import pallas as pl
from jax.experimental.pallas import tpu as pltpu

F32 = jnp.float32
BF16 = jnp.bfloat16

D_MODEL = 1024
HEAD_DIM = 64
A_Q_HEADS = 8
A_KV_HEADS = 2
A_GROUP = A_Q_HEADS // A_KV_HEADS
B_HEADS = 4
B_V_DIM = 2 * HEAD_DIM
GRID_W = 64
ROPE_THETA = 10000.0
EPS = 1e-6
LN_EPS = 1e-5
PEER_HEADS = 8
PEER_N_KEYS = 128
PEER_N_EXPERTS = PEER_N_KEYS * PEER_N_KEYS
PEER_HALF = 128
PEER_TOPK = 16
DEPTH = 1
DN_ALPHA = (2.0 * DEPTH) ** 0.25
LAMBDA_INIT = 0.8 - 0.6 * math.exp(-0.3 * 0)

COLS_A_Q = A_Q_HEADS * HEAD_DIM
COLS_A_KV = A_KV_HEADS * HEAD_DIM
COLS_B_QK = B_HEADS * 2 * HEAD_DIM
COLS_B_V = B_HEADS * B_V_DIM
COLS_QKV = COLS_A_Q + 2 * COLS_A_KV + 2 * COLS_B_QK + COLS_B_V

LANES = 128
BF16_ROWS = 16
VMEM_LIMIT = 48 * 1024 * 1024
LOG2E = math.log2(math.e)

_CAND_PAIRS = tuple((k, l) for k in range(PEER_TOPK + 1) for l in range(PEER_TOPK + 1)
                    if (k + 1) * (l + 1) <= PEER_TOPK + 1)


def _cparams(sem):
    return pltpu.CompilerParams(dimension_semantics=sem, vmem_limit_bytes=VMEM_LIMIT)


def _rope_tables(seq):
    t = np.arange(seq)

    def tab(pos, dim):
        freqs = ROPE_THETA ** (-np.arange(0, dim, 2, dtype=np.float64) / dim)
        ang = pos.astype(np.float64)[:, None] * freqs[None, :]
        return np.cos(ang), np.sin(ang)

    cr, sr = tab(t // GRID_W, HEAD_DIM // 2)
    cc, sc = tab(t % GRID_W, HEAD_DIM // 2)
    ct, st = tab(t, HEAD_DIM)
    cos_a = np.concatenate([cr, cr, cc, cc] * 2, axis=-1)
    sin_a = np.concatenate([-sr, sr, -sc, sc] * 2, axis=-1)
    cos_b = np.concatenate([ct, ct] * 2, axis=-1)
    sin_b = np.concatenate([-st, st] * 2, axis=-1)
    return tuple(jnp.asarray(a, F32) for a in (cos_a, sin_a, cos_b, sin_b))


def _rotate(x, cos, sin_signed, half):
    n = x.shape[-1]
    lane = lax.broadcasted_iota(jnp.int32, x.shape, 1)
    first = (lane & (2 * half - 1)) < half
    partner = jnp.where(first, pltpu.roll(x, n - half, 1), pltpu.roll(x, half, 1))
    return x * cos + partner * sin_signed


def _prep_kernel(x_ref, w_ref, gq_ref, gk_ref, ca_ref, sa_ref, cb_ref, sb_ref,
                 gsum_ref, qa_ref, ka_ref, va_ref, qb_ref, kb_ref, vb_ref):
    xb = x_ref[0].astype(BF16)
    tm = xb.shape[0]
    q_scale = HEAD_DIM ** -0.5 * LOG2E
    ones_rows = (lax.broadcasted_iota(jnp.int32, (BF16_ROWS, tm), 0) == 0).astype(BF16)

    def proj(c0, width):
        return jnp.dot(xb, w_ref[:, c0:c0 + width], preferred_element_type=F32)

    def norm_rope_a(y, gain):
        ss = jnp.dot((y * y).astype(BF16), gsum_ref[...], preferred_element_type=F32)
        y = y * lax.rsqrt(ss * (1.0 / HEAD_DIM) + EPS) * gain
        return _rotate(y, ca_ref[...], sa_ref[...], HEAD_DIM // 4)

    def rope_b(y):
        return _rotate(y, cb_ref[...], sb_ref[...], HEAD_DIM // 2)

    def store_q(ref, y, c):
        yt = (y * q_scale).T.astype(ref.dtype)
        ref[0, 2 * c] = yt[:HEAD_DIM]
        ref[0, 2 * c + 1] = yt[HEAD_DIM:]

    def store_k(ref, y, c):
        y = y.astype(ref.dtype)
        ref[0, 2 * c] = y[:, :HEAD_DIM]
        ref[0, 2 * c + 1] = y[:, HEAD_DIM:]

    c0 = 0
    qa = proj(c0, COLS_A_Q)
    for c in range(COLS_A_Q // LANES):
        store_q(qa_ref, norm_rope_a(qa[:, c * LANES:(c + 1) * LANES], gq_ref[...]), c)
    c0 += COLS_A_Q
    store_k(ka_ref, norm_rope_a(proj(c0, COLS_A_KV), gk_ref[...]), 0)
    c0 += COLS_A_KV
    vat = proj(c0, COLS_A_KV).T.astype(va_ref.dtype)
    for h in range(A_KV_HEADS):
        va_ref[0, h, 0, :HEAD_DIM] = vat[h * HEAD_DIM:(h + 1) * HEAD_DIM]
        va_ref[0, h, 0, HEAD_DIM:] = ones_rows
    c0 += COLS_A_KV
    qb = proj(c0, COLS_B_QK)
    for c in range(COLS_B_QK // LANES):
        store_q(qb_ref, rope_b(qb[:, c * LANES:(c + 1) * LANES]), c)
    c0 += COLS_B_QK
    kb = proj(c0, COLS_B_QK)
    for c in range(COLS_B_QK // LANES):
        store_k(kb_ref, rope_b(kb[:, c * LANES:(c + 1) * LANES]), c)
    c0 += COLS_B_QK
    vb = proj(c0, COLS_B_V)
    for h in range(B_HEADS):
        vb_ref[0, h, 0, :B_V_DIM] = vb[:, h * B_V_DIM:(h + 1) * B_V_DIM].T.astype(vb_ref.dtype)
        vb_ref[0, h, 0, B_V_DIM:] = ones_rows


def _prep(x, w_qkv, gq, gk, tables, tm):
    bsz, seq, _ = x.shape
    nblk = seq // tm
    gsum = jnp.asarray(np.kron(np.eye(LANES // HEAD_DIM), np.ones((HEAD_DIM, HEAD_DIM))), BF16)
    tok = lambda b, i: (i, 0)
    const = lambda b, i: (0, 0)
    qt_out = lambda nh: pl.BlockSpec((1, nh, HEAD_DIM, tm), lambda b, i: (b, 0, 0, i))
    k_out = lambda nh: pl.BlockSpec((1, nh, tm, HEAD_DIM), lambda b, i: (b, 0, i, 0))
    vt_out = lambda nh, r: pl.BlockSpec((1, nh, 1, r, tm), lambda b, i: (b, 0, i, 0, 0))
    rows_a = HEAD_DIM + BF16_ROWS
    rows_b = B_V_DIM + BF16_ROWS
    out_shape = (
        jax.ShapeDtypeStruct((bsz, A_Q_HEADS, HEAD_DIM, seq), BF16),
        jax.ShapeDtypeStruct((bsz, A_KV_HEADS, seq, HEAD_DIM), BF16),
        jax.ShapeDtypeStruct((bsz, A_KV_HEADS, nblk, rows_a, tm), BF16),
        jax.ShapeDtypeStruct((bsz, 2 * B_HEADS, HEAD_DIM, seq), BF16),
        jax.ShapeDtypeStruct((bsz, 2 * B_HEADS, seq, HEAD_DIM), BF16),
        jax.ShapeDtypeStruct((bsz, B_HEADS, nblk, rows_b, tm), BF16),
    )
    return pl.pallas_call(
        _prep_kernel,
        grid=(bsz, nblk),
        in_specs=[
            pl.BlockSpec((1, tm, D_MODEL), lambda b, i: (b, i, 0)),
            pl.BlockSpec((D_MODEL, COLS_QKV), const),
            pl.BlockSpec((1, LANES), const),
            pl.BlockSpec((1, LANES), const),
        ] + [pl.BlockSpec((tm, LANES), tok)] * 4 + [pl.BlockSpec((LANES, LANES), const)],
        out_specs=(qt_out(A_Q_HEADS), k_out(A_KV_HEADS), vt_out(A_KV_HEADS, rows_a),
                   qt_out(2 * B_HEADS), k_out(2 * B_HEADS), vt_out(B_HEADS, rows_b)),
        out_shape=out_shape,
        compiler_params=_cparams(("parallel", "parallel")),
        name="prep",
    )(x, w_qkv, gq, gk, *tables, gsum)


SCORE_ROW_PAD = 2 * LANES


def _flash_kernel(qt_ref, qn_ref, k_ref, vt_ref, o_ref, qcat_ref, qnext_ref, acc_ref, m_ref,
                  s0_ref, s1_ref, c0_ref, c1_ref, *, dv):
    group, _, tq = qt_ref.shape[1:]
    nchunk, _, tk = vt_ref.shape[2:]
    for r in range(group):
        qcat_ref[:, r * tq:(r + 1) * tq] = qt_ref[0, r]
        qnext_ref[:, r * tq:(r + 1) * tq] = qn_ref[0, r]
    acc_ref[...] = jnp.zeros_like(acc_ref)
    m_ref[...] = jnp.full_like(m_ref, -jnp.inf)

    def scores(q_ref, s_ref, c_ref, j):
        start = pl.multiple_of(j * tk, tk)
        k = k_ref[0, 0, pl.ds(start, tk), :]
        st = jnp.dot(k, q_ref[...], preferred_element_type=F32)
        s_ref[:, :st.shape[1]] = st
        c_ref[...] = jnp.max(st, axis=0, keepdims=True)

    def update(s_ref, c_ref, j):
        m_prev = m_ref[...]
        m_new = jnp.maximum(m_prev, c_ref[...])
        pt = jnp.exp2(s_ref[:, :m_new.shape[1]] - m_new).astype(BF16)
        alpha = jnp.exp2(m_prev - m_new)
        acc_ref[...] = alpha * acc_ref[...] + jnp.dot(vt_ref[0, 0, j], pt, preferred_element_type=F32)
        m_ref[...] = m_new

    @pl.when(pl.program_id(1) == 0)
    def _():
        scores(qcat_ref, s0_ref, c0_ref, 0)

    def body(i, carry):
        j = 2 * i
        scores(qcat_ref, s1_ref, c1_ref, j + 1)
        update(s0_ref, c0_ref, j)
        scores(qcat_ref, s0_ref, c0_ref, j + 2)
        update(s1_ref, c1_ref, j + 1)
        return carry

    lax.fori_loop(0, nchunk // 2 - 1, body, 0)
    scores(qcat_ref, s1_ref, c1_ref, nchunk - 1)
    update(s0_ref, c0_ref, nchunk - 2)
    scores(qnext_ref, s0_ref, c0_ref, 0)
    update(s1_ref, c1_ref, nchunk - 1)
    acc = acc_ref[...]
    ot = (acc[:dv] / acc[dv:dv + 1]).astype(o_ref.dtype)
    for r in range(group):
        o_ref[0, r] = ot[:, r * tq:(r + 1) * tq]


def _flash(qt, k, vt, *, v_of, tq, dv):
    n, group, _, seq = qt.shape
    nchunk, rows, tk = vt.shape[2:]
    nq = seq // tq
    q_spec = lambda nxt: pl.BlockSpec((1, group, HEAD_DIM, tq),
                                      lambda b, i: (b, 0, 0, jnp.minimum(i + nxt, nq - 1)))
    return pl.pallas_call(
        functools.partial(_flash_kernel, dv=dv),
        grid=(n, nq),
        in_specs=[
            q_spec(0), q_spec(1),
            pl.BlockSpec((1, 1, seq, HEAD_DIM), lambda b, i: (b, 0, 0, 0)),
            pl.BlockSpec((1, 1, nchunk, rows, tk), lambda b, i: (v_of(b), 0, 0, 0, 0)),
        ],
        out_specs=pl.BlockSpec((1, group, dv, tq), lambda b, i: (b, 0, 0, i)),
        out_shape=jax.ShapeDtypeStruct((n, group, dv, seq), BF16),
        scratch_shapes=[pltpu.VMEM((HEAD_DIM, group * tq), BF16),
                        pltpu.VMEM((HEAD_DIM, group * tq), BF16),
                        pltpu.VMEM((rows, group * tq), F32),
                        pltpu.VMEM((1, group * tq), F32),
                        pltpu.VMEM((tk, group * tq + SCORE_ROW_PAD), F32),
                        pltpu.VMEM((tk, group * tq + SCORE_ROW_PAD), F32),
                        pltpu.VMEM((1, group * tq), F32),
                        pltpu.VMEM((1, group * tq), F32)],
        compiler_params=_cparams(("parallel", "arbitrary")),
        name="flash_dv%d" % dv,
    )(qt, qt, k, vt)


def _layer_norm(x, g, b):
    mu = jnp.mean(x, axis=-1, keepdims=True)
    xc = x - mu
    var = jnp.mean(xc * xc, axis=-1, keepdims=True)
    return xc * lax.rsqrt(var + LN_EPS) * g + b


def _dot_t(at, b):
    return lax.dot_general(at, b, (((0,), (0,)), ((), ())), preferred_element_type=F32)


def _merge_kernel(x_ref, oa_ref, ob_ref, wg_ref, bg_ref, wpa_ref, wpb_ref, wo_ref, sub_ref,
                  lq1_ref, lk1_ref, lq2_ref, lk2_ref, g_ref, b_ref, h_ref, ht_ref):
    x = x_ref[0]
    xb = x.astype(BF16)
    tm = x.shape[0]
    lam = (jnp.exp(jnp.sum(lq1_ref[...] * lk1_ref[...], axis=-1, keepdims=True))
           - jnp.exp(jnp.sum(lq2_ref[...] * lk2_ref[...], axis=-1, keepdims=True)) + LAMBDA_INIT)
    gl = jnp.dot(xb, wg_ref[...], preferred_element_type=F32) + bg_ref[...]
    gate_a = jax.nn.sigmoid(gl[:, :D_MODEL])
    gate_b = jax.nn.sigmoid(gl[:, D_MODEL:])

    pa = _dot_t(oa_ref[0].reshape(COLS_A_Q, tm), wpa_ref[...])
    ybt = []
    for h in range(B_HEADS):
        y = ob_ref[0, 2 * h].astype(F32) - lam * ob_ref[0, 2 * h + 1].astype(F32)
        y = y * lax.rsqrt(jnp.mean(y * y, axis=0, keepdims=True) + EPS) * sub_ref[...]
        ybt.append((y * (1.0 - LAMBDA_INIT)).astype(BF16))
    pb = _dot_t(jnp.concatenate(ybt, axis=0), wpb_ref[...])
    merged = gate_a * pa + gate_b * pb
    pre = DN_ALPHA * x + jnp.dot(merged.astype(BF16), wo_ref[...], preferred_element_type=F32)
    h_out = _layer_norm(pre, g_ref[...], b_ref[...])
    h_ref[0] = h_out
    ht_ref[...] = h_out.T.astype(ht_ref.dtype)


def _merge(x, oat, obt, wg, bg, wpa, wpb, wo, sub, lams, g, b, tm):
    bsz, seq, _ = x.shape
    nblk = seq // tm
    const = lambda bb, i: (0, 0)
    vec = lambda w: pl.BlockSpec((1, w), const)
    return pl.pallas_call(
        _merge_kernel,
        grid=(bsz, nblk),
        in_specs=[
            pl.BlockSpec((1, tm, D_MODEL), lambda bb, i: (bb, i, 0)),
            pl.BlockSpec((1, A_Q_HEADS, HEAD_DIM, tm), lambda bb, i: (bb, 0, 0, i)),
            pl.BlockSpec((1, 2 * B_HEADS, B_V_DIM, tm), lambda bb, i: (bb, 0, 0, i)),
            pl.BlockSpec((D_MODEL, 2 * D_MODEL), const),
            vec(2 * D_MODEL),
            pl.BlockSpec((COLS_A_Q, D_MODEL), const),
            pl.BlockSpec((COLS_B_V, D_MODEL), const),
            pl.BlockSpec((D_MODEL, D_MODEL), const),
            pl.BlockSpec((B_V_DIM, 1), const),
            vec(HEAD_DIM), vec(HEAD_DIM), vec(HEAD_DIM), vec(HEAD_DIM),
            vec(D_MODEL), vec(D_MODEL),
        ],
        out_specs=(
            pl.BlockSpec((1, tm, D_MODEL), lambda bb, i: (bb, i, 0)),
            pl.BlockSpec((D_MODEL, tm), lambda bb, i: (0, bb * nblk + i)),
        ),
        out_shape=(
            jax.ShapeDtypeStruct((bsz, seq, D_MODEL), F32),
            jax.ShapeDtypeStruct((D_MODEL, bsz * seq), BF16),
        ),
        compiler_params=_cparams(("parallel", "parallel")),
        name="merge",
    )(x, oat, obt, wg, bg, wpa, wpb, wo, sub, *lams, g, b)


_NO_RANK = 32.0


SUBLANES = 8


def _sort_network(n):
    pairs, p = [], 1
    while p < n:
        k = p
        while k >= 1:
            for j in range(k % p, n - k, 2 * k):
                for i in range(min(k, n - j - k)):
                    if (i + j) // (2 * p) == (i + j + k) // (2 * p):
                        pairs.append((i + j, i + j + k))
            k //= 2
        p *= 2
    return pairs


def _top_values(s, n):
    vals = []
    if s.shape[0] != PEER_N_KEYS:
        for _ in range(n):
            m = jnp.max(s, axis=0, keepdims=True)
            vals.append(m)
            s = jnp.where(s == m, -jnp.inf, s)
        return vals
    rows = [s[r * SUBLANES:(r + 1) * SUBLANES] for r in range(s.shape[0] // SUBLANES)]
    for a, b in _sort_network(len(rows)):
        rows[a], rows[b] = jnp.maximum(rows[a], rows[b]), jnp.minimum(rows[a], rows[b])
    for i in range(n):
        m = jnp.max(rows[0], axis=0, keepdims=True)
        vals.append(m)
        hit = rows[0] == m
        depth = min(len(rows), n - i)
        below = rows[1:depth + 1]
        below = below + [jnp.full_like(rows[0], -jnp.inf)] * (depth - len(below))
        rows = [jnp.where(hit, below[k], rows[k]) for k in range(depth)]
    return vals


def _rank_codes(s, vals):
    rank = jnp.full(s.shape, _NO_RANK, F32)
    for i, v in enumerate(vals):
        rank = jnp.where(s == v, float(i), rank)
    return rank


def _route_kernel(ht_ref, wq_ref, keys_ref, cnt_ref, coef_ref, rank_ref, e2_ref):
    qt = jnp.dot(wq_ref[...], ht_ref[...], preferred_element_type=F32).astype(BF16)
    s1_all = jnp.dot(keys_ref[0, 0], qt[:PEER_HALF], preferred_element_type=F32)
    s2_all = jnp.dot(keys_ref[0, 1], qt[PEER_HALF:], preferred_element_type=F32)
    for g in range(s1_all.shape[1] // LANES):
        lanes = slice(g * LANES, (g + 1) * LANES)
        s1 = s1_all[:, lanes]
        s2 = s2_all[:, lanes]
        v1 = _top_values(s1, PEER_TOPK + 1)
        v2 = _top_values(s2, PEER_TOPK + 1)
        rank2 = _rank_codes(s2, v2[:PEER_TOPK])
        pair = {(k, l): v1[k] + v2[l] for k, l in _CAND_PAIRS}
        top = _top_values(jnp.concatenate([pair[p] for p in _CAND_PAIRS], axis=0), PEER_TOPK + 1)
        thr = 0.5 * (top[PEER_TOPK - 1] + top[PEER_TOPK])
        ev1 = [jnp.exp(v - v1[0]) for v in v1]
        ev2 = [jnp.exp(v - v2[0]) for v in v2]
        z = jnp.zeros_like(thr)
        pairs = [jnp.zeros_like(thr) for _ in v1]
        for k, l in _CAND_PAIRS:
            sel = pair[(k, l)] > thr
            z = z + jnp.where(sel, ev1[k] * ev2[l], 0.0)
            pairs[k] = pairs[k] + jnp.where(sel, 1.0, 0.0)
        cnt = jnp.zeros_like(s1)
        for k, v in enumerate(v1[:PEER_TOPK]):
            cnt = jnp.where(s1 == v, pairs[k], cnt)
        cnt_ref[0, :, lanes] = cnt
        coef_ref[0, :, lanes] = jnp.exp(s1 - v1[0]) / z
        rank_ref[0, :, lanes] = rank2.astype(rank_ref.dtype)
        e2_ref[0, :, lanes] = jnp.exp(s2 - v2[0]).astype(e2_ref.dtype)


def _route(ht, wq_t, keys, tm):
    ntok = ht.shape[1]
    shape = (PEER_HEADS, PEER_N_KEYS, ntok)
    ospec = pl.BlockSpec((1, PEER_N_KEYS, tm), lambda t, h: (h, 0, t))
    return pl.pallas_call(
        _route_kernel,
        grid=(ntok // tm, PEER_HEADS),
        in_specs=[
            pl.BlockSpec((D_MODEL, tm), lambda t, h: (0, t)),
            pl.BlockSpec((2 * PEER_HALF, D_MODEL), lambda t, h: (h, 0)),
            pl.BlockSpec((1, 2, PEER_N_KEYS, PEER_HALF), lambda t, h: (h, 0, 0, 0)),
        ],
        out_specs=(ospec, ospec, ospec, ospec),
        out_shape=(jax.ShapeDtypeStruct(shape, F32), jax.ShapeDtypeStruct(shape, F32),
                   jax.ShapeDtypeStruct(shape, BF16), jax.ShapeDtypeStruct(shape, BF16)),
        compiler_params=_cparams(("parallel", "parallel")),
        name="route",
    )(ht, wq_t, keys)


def _gelu(a):
    return 0.5 * a * (1.0 + lax.erf(a * (2.0 ** -0.5)))


def _peer_kernel(ht_ref, u_ref, vt_ref, cnt_ref, coef_ref, rank_ref, e2_ref, h_ref, g_ref, b_ref,
                 o_ref, yt_ref, *, rows_per_step):
    e = pl.program_id(1)

    @pl.when(e == 0)
    def _():
        yt_ref[...] = jnp.zeros_like(yt_ref)

    at = jnp.dot(u_ref[...], ht_ref[...], preferred_element_type=F32)
    tm = at.shape[1]
    pieces = []
    for r in range(rows_per_step):
        gate = jnp.zeros((PEER_N_KEYS, tm), BF16)
        for h in range(PEER_HEADS):
            cnt = jnp.broadcast_to(cnt_ref[h, r:r + 1, :], (BF16_ROWS, tm)).astype(BF16)
            coef = jnp.broadcast_to(coef_ref[h, r:r + 1, :], (BF16_ROWS, tm)).astype(BF16)
            cnt = jnp.tile(cnt, (PEER_N_KEYS // BF16_ROWS, 1))
            coef = jnp.tile(coef, (PEER_N_KEYS // BF16_ROWS, 1))
            gate = gate + jnp.where(rank_ref[h] < cnt, e2_ref[h] * coef, jnp.zeros_like(coef))
        a = at[r * PEER_N_KEYS:(r + 1) * PEER_N_KEYS, :]
        pieces.append(_gelu(a.astype(BF16)) * gate)
    wt = jnp.concatenate(pieces, axis=0)
    yt_ref[...] += jnp.dot(vt_ref[...], wt, preferred_element_type=F32)

    @pl.when(e == pl.num_programs(1) - 1)
    def _():
        pre = DN_ALPHA * h_ref[...] + yt_ref[...].T
        o_ref[...] = _layer_norm(pre, g_ref[...], b_ref[...])


def _peer(ht, u, vt, cnt, coef, rank, e2, h, g, b, tm, eb):
    ntok = ht.shape[1]
    rows = eb // PEER_N_KEYS
    full = pl.BlockSpec((PEER_HEADS, PEER_N_KEYS, tm), lambda t, e: (0, 0, t))
    part = pl.BlockSpec((PEER_HEADS, rows, tm), lambda t, e: (0, e, t))
    vec = pl.BlockSpec((1, D_MODEL), lambda t, e: (0, 0))
    return pl.pallas_call(
        functools.partial(_peer_kernel, rows_per_step=rows),
        grid=(ntok // tm, PEER_N_EXPERTS // eb),
        in_specs=[
            pl.BlockSpec((D_MODEL, tm), lambda t, e: (0, t)),
            pl.BlockSpec((eb, D_MODEL), lambda t, e: (e, 0)),
            pl.BlockSpec((D_MODEL, eb), lambda t, e: (0, e)),
            part, part, full, full,
            pl.BlockSpec((tm, D_MODEL), lambda t, e: (t, 0)),
            vec, vec,
        ],
        out_specs=pl.BlockSpec((tm, D_MODEL), lambda t, e: (t, 0)),
        out_shape=jax.ShapeDtypeStruct((ntok, D_MODEL), F32),
        scratch_shapes=[pltpu.VMEM((D_MODEL, tm), F32)],
        compiler_params=_cparams(("parallel", "arbitrary")),
        name="peer",
    )(ht, u, vt, cnt, coef, rank, e2, h, g, b)


def _hybrid_block(x, w_in, b_gate, q_norm_a, k_norm_a, lambda_q1, lambda_k1, lambda_q2, lambda_k2,
                  subln_b, w_proj_a, w_proj_b, w_out, ln1_g, ln1_b, w_query, sub_keys,
                  u_table, v_table, ln2_g, ln2_b, *, tk, tq_a, tq_b, tm_merge, tm_route, tm_peer, eb):
    bsz, seq, _ = x.shape
    row = lambda a: a.reshape(1, -1).astype(F32)
    two = lambda a: jnp.concatenate([row(a), row(a)], axis=-1)

    qat, ka, vat, qbt, kb, vbt = _prep(x, w_in[:, :COLS_QKV].astype(BF16), two(q_norm_a), two(k_norm_a),
                                       _rope_tables(seq), tk)

    nchunk = seq // tk
    oat = _flash(qat.reshape(bsz * A_KV_HEADS, A_GROUP, HEAD_DIM, seq),
                 ka.reshape(bsz * A_KV_HEADS, 1, seq, HEAD_DIM),
                 vat.reshape(bsz * A_KV_HEADS, 1, nchunk, HEAD_DIM + BF16_ROWS, tk),
                 v_of=lambda n: n, tq=tq_a, dv=HEAD_DIM)
    obt = _flash(qbt.reshape(bsz * 2 * B_HEADS, 1, HEAD_DIM, seq),
                 kb.reshape(bsz * 2 * B_HEADS, 1, seq, HEAD_DIM),
                 vbt.reshape(bsz * B_HEADS, 1, nchunk, B_V_DIM + BF16_ROWS, tk),
                 v_of=lambda n: n // 2, tq=tq_b, dv=B_V_DIM)

    h, ht = _merge(x, oat.reshape(bsz, A_Q_HEADS, HEAD_DIM, seq), obt.reshape(bsz, 2 * B_HEADS, B_V_DIM, seq),
                   w_in[:, COLS_QKV:].astype(BF16), row(b_gate), w_proj_a.astype(BF16),
                   w_proj_b.astype(BF16), w_out.astype(BF16), subln_b.reshape(-1, 1).astype(F32),
                   (row(lambda_q1), row(lambda_k1), row(lambda_q2), row(lambda_k2)),
                   row(ln1_g), row(ln1_b), tm_merge)

    cnt, coef, rank, e2 = _route(ht, w_query.T.astype(BF16), sub_keys.astype(BF16), tm_route)
    out = _peer(ht, u_table.astype(BF16), v_table.T.astype(BF16), cnt, coef, rank, e2,
                h.reshape(bsz * seq, D_MODEL), row(ln2_g), row(ln2_b), tm_peer, eb)
    return out.reshape(bsz, seq, D_MODEL)


def kernel(x, w_in, b_gate, q_norm_a, k_norm_a, lambda_q1, lambda_k1, lambda_q2, lambda_k2, subln_b,
           w_proj_a, w_proj_b, w_out, ln1_g, ln1_b, w_query, sub_keys, u_table, v_table, ln2_g, ln2_b):
    return _hybrid_block(
        x, w_in[0], b_gate[0], q_norm_a[0], k_norm_a[0], lambda_q1[0], lambda_k1[0], lambda_q2[0],
        lambda_k2[0], subln_b[0], w_proj_a[0], w_proj_b[0], w_out[0], ln1_g[0], ln1_b[0], w_query[0],
        sub_keys[0], u_table[0], v_table[0], ln2_g[0], ln2_b[0],
        tk=512, tq_a=256, tq_b=1024, tm_merge=512, tm_route=1024, tm_peer=512, eb=2048)
```

```python
import functools
import math

import jax
import jax.numpy as jnp
import numpy as np
from jax import lax
from jax.experimental import pallas as pl
from jax.experimental.pallas import tpu as pltpu

F32 = jnp.float32
BF16 = jnp.bfloat16

D_MODEL = 1024
HEAD_DIM = 64
A_Q_HEADS = 8
A_KV_HEADS = 2
A_GROUP = A_Q_HEADS // A_KV_HEADS
B_HEADS = 4
B_V_DIM = 2 * HEAD_DIM
GRID_W = 64
ROPE_THETA = 10000.0
EPS = 1e-6
LN_EPS = 1e-5
PEER_HEADS = 8
PEER_N_KEYS = 128
PEER_N_EXPERTS = PEER_N_KEYS * PEER_N_KEYS
PEER_HALF = 128
PEER_TOPK = 16
DEPTH = 1
DN_ALPHA = (2.0 * DEPTH) ** 0.25
LAMBDA_INIT = 0.8 - 0.6 * math.exp(-0.3 * 0)

COLS_A_Q = A_Q_HEADS * HEAD_DIM
COLS_A_KV = A_KV_HEADS * HEAD_DIM
COLS_B_QK = B_HEADS * 2 * HEAD_DIM
COLS_B_V = B_HEADS * B_V_DIM
COLS_QKV = COLS_A_Q + 2 * COLS_A_KV + 2 * COLS_B_QK + COLS_B_V

LANES = 128
BF16_ROWS = 16
VMEM_LIMIT = 48 * 1024 * 1024
LOG2E = math.log2(math.e)

_CAND_PAIRS = tuple((k, l) for k in range(PEER_TOPK + 1) for l in range(PEER_TOPK + 1)
                    if (k + 1) * (l + 1) <= PEER_TOPK + 1)


def _cparams(sem):
    return pltpu.CompilerParams(dimension_semantics=sem, vmem_limit_bytes=VMEM_LIMIT)


def _rope_tables(seq):
    t = np.arange(seq)

    def tab(pos, dim):
        freqs = ROPE_THETA ** (-np.arange(0, dim, 2, dtype=np.float64) / dim)
        ang = pos.astype(np.float64)[:, None] * freqs[None, :]
        return np.cos(ang), np.sin(ang)

    cr, sr = tab(t // GRID_W, HEAD_DIM // 2)
    cc, sc = tab(t % GRID_W, HEAD_DIM // 2)
    ct, st = tab(t, HEAD_DIM)
    cos_a = np.concatenate([cr, cr, cc, cc] * 2, axis=-1)
    sin_a = np.concatenate([-sr, sr, -sc, sc] * 2, axis=-1)
    cos_b = np.concatenate([ct, ct] * 2, axis=-1)
    sin_b = np.concatenate([-st, st] * 2, axis=-1)
    return tuple(jnp.asarray(a, F32) for a in (cos_a, sin_a, cos_b, sin_b))


def _rotate(x, cos, sin_signed, half):
    n = x.shape[-1]
    lane = lax.broadcasted_iota(jnp.int32, x.shape, 1)
    first = (lane & (2 * half - 1)) < half
    partner = jnp.where(first, pltpu.roll(x, n - half, 1), pltpu.roll(x, half, 1))
    return x * cos + partner * sin_signed


def _prep_kernel(x_ref, w_ref, gq_ref, gk_ref, ca_ref, sa_ref, cb_ref, sb_ref,
                 gsum_ref, u_ref, v_ref, qa_ref, ka_ref, va_ref, qb_ref, kb_ref, vb_ref, ub_ref, vtab_ref):
    xb = x_ref[0].astype(BF16)
    tm = xb.shape[0]
    q_scale = HEAD_DIM ** -0.5 * LOG2E
    ones_rows = (lax.broadcasted_iota(jnp.int32, (BF16_ROWS, tm), 0) == 0).astype(BF16)

    def proj(c0, width):
        return jnp.dot(xb, w_ref[:, c0:c0 + width], preferred_element_type=F32)

    def norm_rope_a(y, gain):
        ss = jnp.dot((y * y).astype(BF16), gsum_ref[...], preferred_element_type=F32)
        y = y * lax.rsqrt(ss * (1.0 / HEAD_DIM) + EPS) * gain
        return _rotate(y, ca_ref[...], sa_ref[...], HEAD_DIM // 4)

    def rope_b(y):
        return _rotate(y, cb_ref[...], sb_ref[...], HEAD_DIM // 2)

    def store_q(ref, y, c):
        yt = (y * q_scale).T.astype(ref.dtype)
        ref[0, 2 * c] = yt[:HEAD_DIM]
        ref[0, 2 * c + 1] = yt[HEAD_DIM:]

    def store_k(ref, y, c):
        y = y.astype(ref.dtype)
        ref[0, 2 * c] = y[:, :HEAD_DIM]
        ref[0, 2 * c + 1] = y[:, HEAD_DIM:]

    c0 = 0
    qa = proj(c0, COLS_A_Q)
    for c in range(COLS_A_Q // LANES):
        store_q(qa_ref, norm_rope_a(qa[:, c * LANES:(c + 1) * LANES], gq_ref[...]), c)
    c0 += COLS_A_Q
    store_k(ka_ref, norm_rope_a(proj(c0, COLS_A_KV), gk_ref[...]), 0)
    c0 += COLS_A_KV
    vat = proj(c0, COLS_A_KV).T.astype(va_ref.dtype)
    for h in range(A_KV_HEADS):
        va_ref[0, h, 0, :HEAD_DIM] = vat[h * HEAD_DIM:(h + 1) * HEAD_DIM]
        va_ref[0, h, 0, HEAD_DIM:] = ones_rows
    c0 += COLS_A_KV
    qb = proj(c0, COLS_B_QK)
    for c in range(COLS_B_QK // LANES):
        store_q(qb_ref, rope_b(qb[:, c * LANES:(c + 1) * LANES]), c)
    c0 += COLS_B_QK
    kb = proj(c0, COLS_B_QK)
    for c in range(COLS_B_QK // LANES):
        store_k(kb_ref, rope_b(kb[:, c * LANES:(c + 1) * LANES]), c)
    c0 += COLS_B_QK
    vb = proj(c0, COLS_B_V)
    for h in range(B_HEADS):
        vb_ref[0, h, 0, :B_V_DIM] = vb[:, h * B_V_DIM:(h + 1) * B_V_DIM].T.astype(vb_ref.dtype)
        vb_ref[0, h, 0, B_V_DIM:] = ones_rows
    ub_ref[...] = u_ref[...].astype(ub_ref.dtype)
    vtab_ref[...] = v_ref[...].T.astype(vtab_ref.dtype)


def _prep(x, w_qkv, gq, gk, tables, u_table, v_table, tm):
    bsz, seq, _ = x.shape
    nblk = seq // tm
    tab_rows = PEER_N_EXPERTS // (bsz * nblk)
    tab_in = pl.BlockSpec((tab_rows, D_MODEL), lambda b, i: (b * nblk + i, 0))
    gsum = jnp.asarray(np.kron(np.eye(LANES // HEAD_DIM), np.ones((HEAD_DIM, HEAD_DIM))), BF16)
    tok = lambda b, i: (i, 0)
    const = lambda b, i: (0, 0)
    qt_out = lambda nh: pl.BlockSpec((1, nh, HEAD_DIM, tm), lambda b, i: (b, 0, 0, i))
    k_out = lambda nh: pl.BlockSpec((1, nh, tm, HEAD_DIM), lambda b, i: (b, 0, i, 0))
    vt_out = lambda nh, r: pl.BlockSpec((1, nh, 1, r, tm), lambda b, i: (b, 0, i, 0, 0))
    rows_a = HEAD_DIM + BF16_ROWS
    rows_b = B_V_DIM + BF16_ROWS
    out_shape = (
        jax.ShapeDtypeStruct((bsz, A_Q_HEADS, HEAD_DIM, seq), BF16),
        jax.ShapeDtypeStruct((bsz, A_KV_HEADS, seq, HEAD_DIM), BF16),
        jax.ShapeDtypeStruct((bsz, A_KV_HEADS, nblk, rows_a, tm), BF16),
        jax.ShapeDtypeStruct((bsz, 2 * B_HEADS, HEAD_DIM, seq), BF16),
        jax.ShapeDtypeStruct((bsz, 2 * B_HEADS, seq, HEAD_DIM), BF16),
        jax.ShapeDtypeStruct((bsz, B_HEADS, nblk, rows_b, tm), BF16),
        jax.ShapeDtypeStruct((PEER_N_EXPERTS, D_MODEL), BF16),
        jax.ShapeDtypeStruct((D_MODEL, PEER_N_EXPERTS), BF16),
    )
    return pl.pallas_call(
        _prep_kernel,
        grid=(bsz, nblk),
        in_specs=[
            pl.BlockSpec((1, tm, D_MODEL), lambda b, i: (b, i, 0)),
            pl.BlockSpec((D_MODEL, COLS_QKV), const),
            pl.BlockSpec((1, LANES), const),
            pl.BlockSpec((1, LANES), const),
        ] + [pl.BlockSpec((tm, LANES), tok)] * 4 + [pl.BlockSpec((LANES, LANES), const), tab_in, tab_in],
        out_specs=(qt_out(A_Q_HEADS), k_out(A_KV_HEADS), vt_out(A_KV_HEADS, rows_a),
                   qt_out(2 * B_HEADS), k_out(2 * B_HEADS), vt_out(B_HEADS, rows_b),
                   tab_in, pl.BlockSpec((D_MODEL, tab_rows), lambda b, i: (0, b * nblk + i))),
        out_shape=out_shape,
        compiler_params=_cparams(("parallel", "parallel")),
        name="prep",
    )(x, w_qkv, gq, gk, *tables, gsum, u_table, v_table)


SCORE_ROW_PAD = 2 * LANES


def _flash_kernel(qt_ref, qn_ref, k_ref, vt_ref, o_ref, qcat_ref, qnext_ref, acc_ref, m_ref,
                  s0_ref, s1_ref, c0_ref, c1_ref, *, dv):
    group, _, tq = qt_ref.shape[1:]
    nchunk, _, tk = vt_ref.shape[2:]
    for r in range(group):
        qcat_ref[:, r * tq:(r + 1) * tq] = qt_ref[0, r]
        qnext_ref[:, r * tq:(r + 1) * tq] = qn_ref[0, r]
    acc_ref[...] = jnp.zeros_like(acc_ref)
    m_ref[...] = jnp.full_like(m_ref, -jnp.inf)

    def scores(q_ref, s_ref, c_ref, j):
        start = pl.multiple_of(j * tk, tk)
        k = k_ref[0, 0, pl.ds(start, tk), :]
        st = jnp.dot(k, q_ref[...], preferred_element_type=F32)
        s_ref[:, :st.shape[1]] = st
        c_ref[...] = jnp.max(st, axis=0, keepdims=True)

    def update(s_ref, c_ref, j):
        m_prev = m_ref[...]
        m_new = jnp.maximum(m_prev, c_ref[...])
        pt = jnp.exp2(s_ref[:, :m_new.shape[1]] - m_new).astype(BF16)
        alpha = jnp.exp2(m_prev - m_new)
        acc_ref[...] = alpha * acc_ref[...] + jnp.dot(vt_ref[0, 0, j], pt, preferred_element_type=F32)
        m_ref[...] = m_new

    @pl.when(pl.program_id(1) == 0)
    def _():
        scores(qcat_ref, s0_ref, c0_ref, 0)

    def body(i, carry):
        j = 2 * i
        scores(qcat_ref, s1_ref, c1_ref, j + 1)
        update(s0_ref, c0_ref, j)
        scores(qcat_ref, s0_ref, c0_ref, j + 2)
        update(s1_ref, c1_ref, j + 1)
        return carry

    lax.fori_loop(0, nchunk // 2 - 1, body, 0)
    scores(qcat_ref, s1_ref, c1_ref, nchunk - 1)
    update(s0_ref, c0_ref, nchunk - 2)
    scores(qnext_ref, s0_ref, c0_ref, 0)
    update(s1_ref, c1_ref, nchunk - 1)
    acc = acc_ref[...]
    ot = (acc[:dv] / acc[dv:dv + 1]).astype(o_ref.dtype)
    for r in range(group):
        o_ref[0, r] = ot[:, r * tq:(r + 1) * tq]


def _flash(qt, k, vt, *, v_of, tq, dv):
    n, group, _, seq = qt.shape
    nchunk, rows, tk = vt.shape[2:]
    nq = seq // tq
    q_spec = lambda nxt: pl.BlockSpec((1, group, HEAD_DIM, tq),
                                      lambda b, i: (b, 0, 0, jnp.minimum(i + nxt, nq - 1)))
    return pl.pallas_call(
        functools.partial(_flash_kernel, dv=dv),
        grid=(n, nq),
        in_specs=[
            q_spec(0), q_spec(1),
            pl.BlockSpec((1, 1, seq, HEAD_DIM), lambda b, i: (b, 0, 0, 0)),
            pl.BlockSpec((1, 1, nchunk, rows, tk), lambda b, i: (v_of(b), 0, 0, 0, 0)),
        ],
        out_specs=pl.BlockSpec((1, group, dv, tq), lambda b, i: (b, 0, 0, i)),
        out_shape=jax.ShapeDtypeStruct((n, group, dv, seq), BF16),
        scratch_shapes=[pltpu.VMEM((HEAD_DIM, group * tq), BF16),
                        pltpu.VMEM((HEAD_DIM, group * tq), BF16),
                        pltpu.VMEM((rows, group * tq), F32),
                        pltpu.VMEM((1, group * tq), F32),
                        pltpu.VMEM((tk, group * tq + SCORE_ROW_PAD), F32),
                        pltpu.VMEM((tk, group * tq + SCORE_ROW_PAD), F32),
                        pltpu.VMEM((1, group * tq), F32),
                        pltpu.VMEM((1, group * tq), F32)],
        compiler_params=_cparams(("parallel", "arbitrary")),
        name="flash_dv%d" % dv,
    )(qt, qt, k, vt)


def _layer_norm(x, g, b):
    mu = jnp.mean(x, axis=-1, keepdims=True)
    xc = x - mu
    var = jnp.mean(xc * xc, axis=-1, keepdims=True)
    return xc * lax.rsqrt(var + LN_EPS) * g + b


def _dot_t(at, b):
    return lax.dot_general(at, b, (((0,), (0,)), ((), ())), preferred_element_type=F32)


def _merge_kernel(x_ref, oa_ref, ob_ref, wg_ref, bg_ref, wpa_ref, wpb_ref, wo_ref, sub_ref,
                  lq1_ref, lk1_ref, lq2_ref, lk2_ref, g_ref, b_ref, h_ref, ht_ref):
    x = x_ref[0]
    xb = x.astype(BF16)
    tm = x.shape[0]
    lam = (jnp.exp(jnp.sum(lq1_ref[...] * lk1_ref[...], axis=-1, keepdims=True))
           - jnp.exp(jnp.sum(lq2_ref[...] * lk2_ref[...], axis=-1, keepdims=True)) + LAMBDA_INIT)
    gl = jnp.dot(xb, wg_ref[...], preferred_element_type=F32) + bg_ref[...]
    gate_a = jax.nn.sigmoid(gl[:, :D_MODEL])
    gate_b = jax.nn.sigmoid(gl[:, D_MODEL:])

    pa = _dot_t(oa_ref[0].reshape(COLS_A_Q, tm), wpa_ref[...])
    ybt = []
    for h in range(B_HEADS):
        y = ob_ref[0, 2 * h].astype(F32) - lam * ob_ref[0, 2 * h + 1].astype(F32)
        y = y * lax.rsqrt(jnp.mean(y * y, axis=0, keepdims=True) + EPS) * sub_ref[...]
        ybt.append((y * (1.0 - LAMBDA_INIT)).astype(BF16))
    pb = _dot_t(jnp.concatenate(ybt, axis=0), wpb_ref[...])
    merged = gate_a * pa + gate_b * pb
    pre = DN_ALPHA * x + jnp.dot(merged.astype(BF16), wo_ref[...], preferred_element_type=F32)
    h_out = _layer_norm(pre, g_ref[...], b_ref[...])
    h_ref[0] = h_out
    ht_ref[...] = h_out.T.astype(ht_ref.dtype)


def _merge(x, oat, obt, wg, bg, wpa, wpb, wo, sub, lams, g, b, tm):
    bsz, seq, _ = x.shape
    nblk = seq // tm
    const = lambda bb, i: (0, 0)
    vec = lambda w: pl.BlockSpec((1, w), const)
    return pl.pallas_call(
        _merge_kernel,
        grid=(bsz, nblk),
        in_specs=[
            pl.BlockSpec((1, tm, D_MODEL), lambda bb, i: (bb, i, 0)),
            pl.BlockSpec((1, A_Q_HEADS, HEAD_DIM, tm), lambda bb, i: (bb, 0, 0, i)),
            pl.BlockSpec((1, 2 * B_HEADS, B_V_DIM, tm), lambda bb, i: (bb, 0, 0, i)),
            pl.BlockSpec((D_MODEL, 2 * D_MODEL), const),
            vec(2 * D_MODEL),
            pl.BlockSpec((COLS_A_Q, D_MODEL), const),
            pl.BlockSpec((COLS_B_V, D_MODEL), const),
            pl.BlockSpec((D_MODEL, D_MODEL), const),
            pl.BlockSpec((B_V_DIM, 1), const),
            vec(HEAD_DIM), vec(HEAD_DIM), vec(HEAD_DIM), vec(HEAD_DIM),
            vec(D_MODEL), vec(D_MODEL),
        ],
        out_specs=(
            pl.BlockSpec((1, tm, D_MODEL), lambda bb, i: (bb, i, 0)),
            pl.BlockSpec((D_MODEL, tm), lambda bb, i: (0, bb * nblk + i)),
        ),
        out_shape=(
            jax.ShapeDtypeStruct((bsz, seq, D_MODEL), F32),
            jax.ShapeDtypeStruct((D_MODEL, bsz * seq), BF16),
        ),
        compiler_params=_cparams(("parallel", "parallel")),
        name="merge",
    )(x, oat, obt, wg, bg, wpa, wpb, wo, sub, *lams, g, b)


_NO_RANK = 32.0


SUBLANES = 8


def _sort_network(n):
    pairs, p = [], 1
    while p < n:
        k = p
        while k >= 1:
            for j in range(k % p, n - k, 2 * k):
                for i in range(min(k, n - j - k)):
                    if (i + j) // (2 * p) == (i + j + k) // (2 * p):
                        pairs.append((i + j, i + j + k))
            k //= 2
        p *= 2
    return pairs


def _top_values(s, n):
    vals = []
    if s.shape[0] != PEER_N_KEYS:
        for _ in range(n):
            m = jnp.max(s, axis=0, keepdims=True)
            vals.append(m)
            s = jnp.where(s == m, -jnp.inf, s)
        return vals
    rows = [s[r * SUBLANES:(r + 1) * SUBLANES] for r in range(s.shape[0] // SUBLANES)]
    for a, b in _sort_network(len(rows)):
        rows[a], rows[b] = jnp.maximum(rows[a], rows[b]), jnp.minimum(rows[a], rows[b])
    for i in range(n):
        m = jnp.max(rows[0], axis=0, keepdims=True)
        vals.append(m)
        hit = rows[0] == m
        depth = min(len(rows), n - i)
        below = rows[1:depth + 1]
        below = below + [jnp.full_like(rows[0], -jnp.inf)] * (depth - len(below))
        rows = [jnp.where(hit, below[k], rows[k]) for k in range(depth)]
    return vals


def _rank_codes(s, vals):
    rank = jnp.full(s.shape, _NO_RANK, F32)
    for i, v in enumerate(vals):
        rank = jnp.where(s == v, float(i), rank)
    return rank


def _route_kernel(ht_ref, wq_ref, keys_ref, cnt_ref, coef_ref, rank_ref, e2_ref):
    qt = jnp.dot(wq_ref[...], ht_ref[...], preferred_element_type=F32).astype(BF16)
    s1_all = jnp.dot(keys_ref[0, 0], qt[:PEER_HALF], preferred_element_type=F32)
    s2_all = jnp.dot(keys_ref[0, 1], qt[PEER_HALF:], preferred_element_type=F32)
    for g in range(s1_all.shape[1] // LANES):
        lanes = slice(g * LANES, (g + 1) * LANES)
        s1 = s1_all[:, lanes]
        s2 = s2_all[:, lanes]
        v1 = _top_values(s1, PEER_TOPK + 1)
        v2 = _top_values(s2, PEER_TOPK + 1)
        rank2 = _rank_codes(s2, v2[:PEER_TOPK])
        pair = {(k, l): v1[k] + v2[l] for k, l in _CAND_PAIRS}
        top = _top_values(jnp.concatenate([pair[p] for p in _CAND_PAIRS], axis=0), PEER_TOPK + 1)
        thr = 0.5 * (top[PEER_TOPK - 1] + top[PEER_TOPK])
        ev1 = [jnp.exp(v - v1[0]) for v in v1]
        ev2 = [jnp.exp(v - v2[0]) for v in v2]
        z = jnp.zeros_like(thr)
        pairs = [jnp.zeros_like(thr) for _ in v1]
        for k, l in _CAND_PAIRS:
            sel = pair[(k, l)] > thr
            z = z + jnp.where(sel, ev1[k] * ev2[l], 0.0)
            pairs[k] = pairs[k] + jnp.where(sel, 1.0, 0.0)
        cnt = jnp.zeros_like(s1)
        for k, v in enumerate(v1[:PEER_TOPK]):
            cnt = jnp.where(s1 == v, pairs[k], cnt)
        cnt_ref[0, :, lanes] = cnt
        coef_ref[0, :, lanes] = jnp.exp(s1 - v1[0]) / z
        rank_ref[0, :, lanes] = rank2.astype(rank_ref.dtype)
        e2_ref[0, :, lanes] = jnp.exp(s2 - v2[0]).astype(e2_ref.dtype)


def _route(ht, wq_t, keys, tm):
    ntok = ht.shape[1]
    shape = (PEER_HEADS, PEER_N_KEYS, ntok)
    ospec = pl.BlockSpec((1, PEER_N_KEYS, tm), lambda t, h: (h, 0, t))
    return pl.pallas_call(
        _route_kernel,
        grid=(ntok // tm, PEER_HEADS),
        in_specs=[
            pl.BlockSpec((D_MODEL, tm), lambda t, h: (0, t)),
            pl.BlockSpec((2 * PEER_HALF, D_MODEL), lambda t, h: (h, 0)),
            pl.BlockSpec((1, 2, PEER_N_KEYS, PEER_HALF), lambda t, h: (h, 0, 0, 0)),
        ],
        out_specs=(ospec, ospec, ospec, ospec),
        out_shape=(jax.ShapeDtypeStruct(shape, F32), jax.ShapeDtypeStruct(shape, F32),
                   jax.ShapeDtypeStruct(shape, BF16), jax.ShapeDtypeStruct(shape, BF16)),
        compiler_params=_cparams(("parallel", "parallel")),
        name="route",
    )(ht, wq_t, keys)


def _gelu(a):
    return 0.5 * a * (1.0 + lax.erf(a * (2.0 ** -0.5)))


def _peer_kernel(ht_ref, u_ref, vt_ref, cnt_ref, coef_ref, rank_ref, e2_ref, h_ref, g_ref, b_ref,
                 o_ref, yt_ref, *, rows_per_step):
    e = pl.program_id(1)

    @pl.when(e == 0)
    def _():
        yt_ref[...] = jnp.zeros_like(yt_ref)

    at = jnp.dot(u_ref[...], ht_ref[...], preferred_element_type=F32)
    tm = at.shape[1]
    pieces = []
    for r in range(rows_per_step):
        gate = jnp.zeros((PEER_N_KEYS, tm), BF16)
        for h in range(PEER_HEADS):
            cnt = jnp.broadcast_to(cnt_ref[h, r:r + 1, :], (BF16_ROWS, tm)).astype(BF16)
            coef = jnp.broadcast_to(coef_ref[h, r:r + 1, :], (BF16_ROWS, tm)).astype(BF16)
            cnt = jnp.tile(cnt, (PEER_N_KEYS // BF16_ROWS, 1))
            coef = jnp.tile(coef, (PEER_N_KEYS // BF16_ROWS, 1))
            gate = gate + jnp.where(rank_ref[h] < cnt, e2_ref[h] * coef, jnp.zeros_like(coef))
        a = at[r * PEER_N_KEYS:(r + 1) * PEER_N_KEYS, :]
        pieces.append(_gelu(a.astype(BF16)) * gate)
    wt = jnp.concatenate(pieces, axis=0)
    yt_ref[...] += jnp.dot(vt_ref[...], wt, preferred_element_type=F32)

    @pl.when(e == pl.num_programs(1) - 1)
    def _():
        pre = DN_ALPHA * h_ref[...] + yt_ref[...].T
        o_ref[...] = _layer_norm(pre, g_ref[...], b_ref[...])


def _peer(ht, u, vt, cnt, coef, rank, e2, h, g, b, tm, eb):
    ntok = ht.shape[1]
    rows = eb // PEER_N_KEYS
    full = pl.BlockSpec((PEER_HEADS, PEER_N_KEYS, tm), lambda t, e: (0, 0, t))
    part = pl.BlockSpec((PEER_HEADS, rows, tm), lambda t, e: (0, e, t))
    vec = pl.BlockSpec((1, D_MODEL), lambda t, e: (0, 0))
    return pl.pallas_call(
        functools.partial(_peer_kernel, rows_per_step=rows),
        grid=(ntok // tm, PEER_N_EXPERTS // eb),
        in_specs=[
            pl.BlockSpec((D_MODEL, tm), lambda t, e: (0, t)),
            pl.BlockSpec((eb, D_MODEL), lambda t, e: (e, 0)),
            pl.BlockSpec((D_MODEL, eb), lambda t, e: (0, e)),
            part, part, full, full,
            pl.BlockSpec((tm, D_MODEL), lambda t, e: (t, 0)),
            vec, vec,
        ],
        out_specs=pl.BlockSpec((tm, D_MODEL), lambda t, e: (t, 0)),
        out_shape=jax.ShapeDtypeStruct((ntok, D_MODEL), F32),
        scratch_shapes=[pltpu.VMEM((D_MODEL, tm), F32)],
        compiler_params=_cparams(("parallel", "arbitrary")),
        name="peer",
    )(ht, u, vt, cnt, coef, rank, e2, h, g, b)


def _hybrid_block(x, w_in, b_gate, q_norm_a, k_norm_a, lambda_q1, lambda_k1, lambda_q2, lambda_k2,
                  subln_b, w_proj_a, w_proj_b, w_out, ln1_g, ln1_b, w_query, sub_keys,
                  u_table, v_table, ln2_g, ln2_b, *, tk, tq_a, tq_b, tm_merge, tm_route, tm_peer, eb):
    bsz, seq, _ = x.shape
    row = lambda a: a.reshape(1, -1).astype(F32)
    two = lambda a: jnp.concatenate([row(a), row(a)], axis=-1)

    qat, ka, vat, qbt, kb, vbt, u_bf, vt_bf = _prep(x, w_in[:, :COLS_QKV].astype(BF16), two(q_norm_a),
                                                    two(k_norm_a), _rope_tables(seq), u_table, v_table, tk)

    nchunk = seq // tk
    oat = _flash(qat.reshape(bsz * A_KV_HEADS, A_GROUP, HEAD_DIM, seq),
                 ka.reshape(bsz * A_KV_HEADS, 1, seq, HEAD_DIM),
                 vat.reshape(bsz * A_KV_HEADS, 1, nchunk, HEAD_DIM + BF16_ROWS, tk),
                 v_of=lambda n: n, tq=tq_a, dv=HEAD_DIM)
    obt = _flash(qbt.reshape(bsz * 2 * B_HEADS, 1, HEAD_DIM, seq),
                 kb.reshape(bsz * 2 * B_HEADS, 1, seq, HEAD_DIM),
                 vbt.reshape(bsz * B_HEADS, 1, nchunk, B_V_DIM + BF16_ROWS, tk),
                 v_of=lambda n: n // 2, tq=tq_b, dv=B_V_DIM)

    h, ht = _merge(x, oat.reshape(bsz, A_Q_HEADS, HEAD_DIM, seq), obt.reshape(bsz, 2 * B_HEADS, B_V_DIM, seq),
                   w_in[:, COLS_QKV:].astype(BF16), row(b_gate), w_proj_a.astype(BF16),
                   w_proj_b.astype(BF16), w_out.astype(BF16), subln_b.reshape(-1, 1).astype(F32),
                   (row(lambda_q1), row(lambda_k1), row(lambda_q2), row(lambda_k2)),
                   row(ln1_g), row(ln1_b), tm_merge)

    cnt, coef, rank, e2 = _route(ht, w_query.T.astype(BF16), sub_keys.astype(BF16), tm_route)
    out = _peer(ht, u_bf, vt_bf, cnt, coef, rank, e2,
                h.reshape(bsz * seq, D_MODEL), row(ln2_g), row(ln2_b), tm_peer, eb)
    return out.reshape(bsz, seq, D_MODEL)


def kernel(x, w_in, b_gate, q_norm_a, k_norm_a, lambda_q1, lambda_k1, lambda_q2, lambda_k2, subln_b,
           w_proj_a, w_proj_b, w_out, ln1_g, ln1_b, w_query, sub_keys, u_table, v_table, ln2_g, ln2_b):
    return _hybrid_block(
        x, w_in[0], b_gate[0], q_norm_a[0], k_norm_a[0], lambda_q1[0], lambda_k1[0], lambda_q2[0],
        lambda_k2[0], subln_b[0], w_proj_a[0], w_proj_b[0], w_out[0], ln1_g[0], ln1_b[0], w_query[0],
        sub_keys[0], u_table[0], v_table[0], ln2_g[0], ln2_b[0],
        tk=512, tq_a=256, tq_b=1024, tm_merge=512, tm_route=1024, tm_peer=512, eb=2048)
```

```python
import functools
import math

import jax
import jax.numpy as jnp
import numpy as np
from jax import lax
from jax.experimental import pallas as pl
from jax.experimental.pallas import tpu as pltpu

F32 = jnp.float32
BF16 = jnp.bfloat16

D_MODEL = 1024
HEAD_DIM = 64
A_Q_HEADS = 8
A_KV_HEADS = 2
A_GROUP = A_Q_HEADS // A_KV_HEADS
B_HEADS = 4
B_V_DIM = 2 * HEAD_DIM
GRID_W = 64
ROPE_THETA = 10000.0
EPS = 1e-6
LN_EPS = 1e-5
PEER_HEADS = 8
PEER_N_KEYS = 128
PEER_N_EXPERTS = PEER_N_KEYS * PEER_N_KEYS
PEER_HALF = 128
PEER_TOPK = 16
DEPTH = 1
DN_ALPHA = (2.0 * DEPTH) ** 0.25
LAMBDA_INIT = 0.8 - 0.6 * math.exp(-0.3 * 0)

COLS_A_Q = A_Q_HEADS * HEAD_DIM
COLS_A_KV = A_KV_HEADS * HEAD_DIM
COLS_B_QK = B_HEADS * 2 * HEAD_DIM
COLS_B_V = B_HEADS * B_V_DIM
COLS_QKV = COLS_A_Q + 2 * COLS_A_KV + 2 * COLS_B_QK + COLS_B_V

LANES = 128
BF16_ROWS = 16
VMEM_LIMIT = 48 * 1024 * 1024
LOG2E = math.log2(math.e)

_CAND_PAIRS = tuple((k, l) for k in range(PEER_TOPK + 1) for l in range(PEER_TOPK + 1)
                    if (k + 1) * (l + 1) <= PEER_TOPK + 1)


def _cparams(sem):
    return pltpu.CompilerParams(dimension_semantics=sem, vmem_limit_bytes=VMEM_LIMIT)


def _rope_tables(seq):
    t = np.arange(seq)

    def tab(pos, dim):
        freqs = ROPE_THETA ** (-np.arange(0, dim, 2, dtype=np.float64) / dim)
        ang = pos.astype(np.float64)[:, None] * freqs[None, :]
        return np.cos(ang), np.sin(ang)

    cr, sr = tab(t // GRID_W, HEAD_DIM // 2)
    cc, sc = tab(t % GRID_W, HEAD_DIM // 2)
    ct, st = tab(t, HEAD_DIM)
    cos_a = np.concatenate([cr, cr, cc, cc] * 2, axis=-1)
    sin_a = np.concatenate([-sr, sr, -sc, sc] * 2, axis=-1)
    cos_b = np.concatenate([ct, ct] * 2, axis=-1)
    sin_b = np.concatenate([-st, st] * 2, axis=-1)
    return tuple(jnp.asarray(a, F32) for a in (cos_a, sin_a, cos_b, sin_b))


def _rotate(x, cos, sin_signed, half):
    n = x.shape[-1]
    lane = lax.broadcasted_iota(jnp.int32, x.shape, 1)
    first = (lane & (2 * half - 1)) < half
    partner = jnp.where(first, pltpu.roll(x, n - half, 1), pltpu.roll(x, half, 1))
    return x * cos + partner * sin_signed


def _prep_kernel(x_ref, w_ref, gq_ref, gk_ref, ca_ref, sa_ref, cb_ref, sb_ref,
                 gsum_ref, u_ref, v_ref, qa_ref, ka_ref, va_ref, qb_ref, kb_ref, vb_ref, ub_ref, vtab_ref):
    xb = x_ref[0].astype(BF16)
    tm = xb.shape[0]
    q_scale = HEAD_DIM ** -0.5 * LOG2E
    ones_rows = (lax.broadcasted_iota(jnp.int32, (BF16_ROWS, tm), 0) == 0).astype(BF16)

    def proj(c0, width):
        return jnp.dot(xb, w_ref[:, c0:c0 + width], preferred_element_type=F32)

    def norm_rope_a(y, gain):
        ss = jnp.dot((y * y).astype(BF16), gsum_ref[...], preferred_element_type=F32)
        y = y * lax.rsqrt(ss * (1.0 / HEAD_DIM) + EPS) * gain
        return _rotate(y, ca_ref[...], sa_ref[...], HEAD_DIM // 4)

    def rope_b(y):
        return _rotate(y, cb_ref[...], sb_ref[...], HEAD_DIM // 2)

    def store_q(ref, y, c):
        yt = (y * q_scale).T.astype(ref.dtype)
        ref[0, 2 * c] = yt[:HEAD_DIM]
        ref[0, 2 * c + 1] = yt[HEAD_DIM:]

    def store_k(ref, y, c):
        y = y.astype(ref.dtype)
        ref[0, 2 * c] = y[:, :HEAD_DIM]
        ref[0, 2 * c + 1] = y[:, HEAD_DIM:]

    c0 = 0
    qa = proj(c0, COLS_A_Q)
    for c in range(COLS_A_Q // LANES):
        store_q(qa_ref, norm_rope_a(qa[:, c * LANES:(c + 1) * LANES], gq_ref[...]), c)
    c0 += COLS_A_Q
    store_k(ka_ref, norm_rope_a(proj(c0, COLS_A_KV), gk_ref[...]), 0)
    c0 += COLS_A_KV
    vat = proj(c0, COLS_A_KV).T.astype(va_ref.dtype)
    for h in range(A_KV_HEADS):
        va_ref[0, h, 0, :HEAD_DIM] = vat[h * HEAD_DIM:(h + 1) * HEAD_DIM]
        va_ref[0, h, 0, HEAD_DIM:] = ones_rows
    c0 += COLS_A_KV
    qb = proj(c0, COLS_B_QK)
    for c in range(COLS_B_QK // LANES):
        store_q(qb_ref, rope_b(qb[:, c * LANES:(c + 1) * LANES]), c)
    c0 += COLS_B_QK
    kb = proj(c0, COLS_B_QK)
    for c in range(COLS_B_QK // LANES):
        store_k(kb_ref, rope_b(kb[:, c * LANES:(c + 1) * LANES]), c)
    c0 += COLS_B_QK
    vb = proj(c0, COLS_B_V)
    for h in range(B_HEADS):
        vb_ref[0, h, 0, :B_V_DIM] = vb[:, h * B_V_DIM:(h + 1) * B_V_DIM].T.astype(vb_ref.dtype)
        vb_ref[0, h, 0, B_V_DIM:] = ones_rows
    ub_ref[...] = u_ref[...].astype(ub_ref.dtype)
    vtab_ref[...] = v_ref[...].T.astype(vtab_ref.dtype)


def _prep(x, w_qkv, gq, gk, tables, u_table, v_table, tm):
    bsz, seq, _ = x.shape
    nblk = seq // tm
    assert PEER_N_EXPERTS % (bsz * nblk) == 0
    tab_rows = PEER_N_EXPERTS // (bsz * nblk)
    tab_in = pl.BlockSpec((tab_rows, D_MODEL), lambda b, i: (b * nblk + i, 0))
    gsum = jnp.asarray(np.kron(np.eye(LANES // HEAD_DIM), np.ones((HEAD_DIM, HEAD_DIM))), BF16)
    tok = lambda b, i: (i, 0)
    const = lambda b, i: (0, 0)
    qt_out = lambda nh: pl.BlockSpec((1, nh, HEAD_DIM, tm), lambda b, i: (b, 0, 0, i))
    k_out = lambda nh: pl.BlockSpec((1, nh, tm, HEAD_DIM), lambda b, i: (b, 0, i, 0))
    vt_out = lambda nh, r: pl.BlockSpec((1, nh, 1, r, tm), lambda b, i: (b, 0, i, 0, 0))
    rows_a = HEAD_DIM + BF16_ROWS
    rows_b = B_V_DIM + BF16_ROWS
    out_shape = (
        jax.ShapeDtypeStruct((bsz, A_Q_HEADS, HEAD_DIM, seq), BF16),
        jax.ShapeDtypeStruct((bsz, A_KV_HEADS, seq, HEAD_DIM), BF16),
        jax.ShapeDtypeStruct((bsz, A_KV_HEADS, nblk, rows_a, tm), BF16),
        jax.ShapeDtypeStruct((bsz, 2 * B_HEADS, HEAD_DIM, seq), BF16),
        jax.ShapeDtypeStruct((bsz, 2 * B_HEADS, seq, HEAD_DIM), BF16),
        jax.ShapeDtypeStruct((bsz, B_HEADS, nblk, rows_b, tm), BF16),
        jax.ShapeDtypeStruct((PEER_N_EXPERTS, D_MODEL), BF16),
        jax.ShapeDtypeStruct((D_MODEL, PEER_N_EXPERTS), BF16),
    )
    return pl.pallas_call(
        _prep_kernel,
        grid=(bsz, nblk),
        in_specs=[
            pl.BlockSpec((1, tm, D_MODEL), lambda b, i: (b, i, 0)),
            pl.BlockSpec((D_MODEL, COLS_QKV), const),
            pl.BlockSpec((1, LANES), const),
            pl.BlockSpec((1, LANES), const),
        ] + [pl.BlockSpec((tm, LANES), tok)] * 4 + [pl.BlockSpec((LANES, LANES), const), tab_in, tab_in],
        out_specs=(qt_out(A_Q_HEADS), k_out(A_KV_HEADS), vt_out(A_KV_HEADS, rows_a),
                   qt_out(2 * B_HEADS), k_out(2 * B_HEADS), vt_out(B_HEADS, rows_b),
                   tab_in, pl.BlockSpec((D_MODEL, tab_rows), lambda b, i: (0, b * nblk + i))),
        out_shape=out_shape,
        compiler_params=_cparams(("parallel", "parallel")),
        name="prep",
    )(x, w_qkv, gq, gk, *tables, gsum, u_table, v_table)


SCORE_ROW_PAD = 2 * LANES


def _flash_kernel(qt_ref, qn_ref, k_ref, vt_ref, o_ref, qcat_ref, qnext_ref, acc_ref, m_ref,
                  s0_ref, s1_ref, c0_ref, c1_ref, *, dv):
    group, _, tq = qt_ref.shape[1:]
    nchunk, _, tk = vt_ref.shape[2:]
    for r in range(group):
        qcat_ref[:, r * tq:(r + 1) * tq] = qt_ref[0, r]
        qnext_ref[:, r * tq:(r + 1) * tq] = qn_ref[0, r]
    acc_ref[...] = jnp.zeros_like(acc_ref)
    m_ref[...] = jnp.full_like(m_ref, -jnp.inf)

    def scores(q_ref, s_ref, c_ref, j):
        start = pl.multiple_of(j * tk, tk)
        k = k_ref[0, 0, pl.ds(start, tk), :]
        st = jnp.dot(k, q_ref[...], preferred_element_type=F32)
        s_ref[:, :st.shape[1]] = st
        c_ref[...] = jnp.max(st, axis=0, keepdims=True)

    def update(s_ref, c_ref, j):
        m_prev = m_ref[...]
        m_new = jnp.maximum(m_prev, c_ref[...])
        pt = jnp.exp2(s_ref[:, :m_new.shape[1]] - m_new).astype(BF16)
        alpha = jnp.exp2(m_prev - m_new)
        acc_ref[...] = alpha * acc_ref[...] + jnp.dot(vt_ref[0, 0, j], pt, preferred_element_type=F32)
        m_ref[...] = m_new

    @pl.when(pl.program_id(1) == 0)
    def _():
        scores(qcat_ref, s0_ref, c0_ref, 0)

    def body(i, carry):
        j = 2 * i
        scores(qcat_ref, s1_ref, c1_ref, j + 1)
        update(s0_ref, c0_ref, j)
        scores(qcat_ref, s0_ref, c0_ref, j + 2)
        update(s1_ref, c1_ref, j + 1)
        return carry

    lax.fori_loop(0, nchunk // 2 - 1, body, 0)
    scores(qcat_ref, s1_ref, c1_ref, nchunk - 1)
    update(s0_ref, c0_ref, nchunk - 2)
    scores(qnext_ref, s0_ref, c0_ref, 0)
    update(s1_ref, c1_ref, nchunk - 1)
    acc = acc_ref[...]
    ot = (acc[:dv] / acc[dv:dv + 1]).astype(o_ref.dtype)
    for r in range(group):
        o_ref[0, r] = ot[:, r * tq:(r + 1) * tq]


def _flash(qt, k, vt, *, v_of, tq, dv):
    n, group, _, seq = qt.shape
    nchunk, rows, tk = vt.shape[2:]
    nq = seq // tq
    q_spec = lambda nxt: pl.BlockSpec((1, group, HEAD_DIM, tq),
                                      lambda b, i: (b, 0, 0, jnp.minimum(i + nxt, nq - 1)))
    return pl.pallas_call(
        functools.partial(_flash_kernel, dv=dv),
        grid=(n, nq),
        in_specs=[
            q_spec(0), q_spec(1),
            pl.BlockSpec((1, 1, seq, HEAD_DIM), lambda b, i: (b, 0, 0, 0)),
            pl.BlockSpec((1, 1, nchunk, rows, tk), lambda b, i: (v_of(b), 0, 0, 0, 0)),
        ],
        out_specs=pl.BlockSpec((1, group, dv, tq), lambda b, i: (b, 0, 0, i)),
        out_shape=jax.ShapeDtypeStruct((n, group, dv, seq), BF16),
        scratch_shapes=[pltpu.VMEM((HEAD_DIM, group * tq), BF16),
                        pltpu.VMEM((HEAD_DIM, group * tq), BF16),
                        pltpu.VMEM((rows, group * tq), F32),
                        pltpu.VMEM((1, group * tq), F32),
                        pltpu.VMEM((tk, group * tq + SCORE_ROW_PAD), F32),
                        pltpu.VMEM((tk, group * tq + SCORE_ROW_PAD), F32),
                        pltpu.VMEM((1, group * tq), F32),
                        pltpu.VMEM((1, group * tq), F32)],
        compiler_params=_cparams(("parallel", "arbitrary")),
        name="flash_dv%d" % dv,
    )(qt, qt, k, vt)


def _layer_norm(x, g, b):
    mu = jnp.mean(x, axis=-1, keepdims=True)
    xc = x - mu
    var = jnp.mean(xc * xc, axis=-1, keepdims=True)
    return xc * lax.rsqrt(var + LN_EPS) * g + b


def _dot_t(at, b):
    return lax.dot_general(at, b, (((0,), (0,)), ((), ())), preferred_element_type=F32)


def _merge_kernel(x_ref, oa_ref, ob_ref, wg_ref, bg_ref, wpa_ref, wpb_ref, wo_ref, sub_ref,
                  lq1_ref, lk1_ref, lq2_ref, lk2_ref, g_ref, b_ref, h_ref, ht_ref):
    x = x_ref[0]
    xb = x.astype(BF16)
    tm = x.shape[0]
    lam = (jnp.exp(jnp.sum(lq1_ref[...] * lk1_ref[...], axis=-1, keepdims=True))
           - jnp.exp(jnp.sum(lq2_ref[...] * lk2_ref[...], axis=-1, keepdims=True)) + LAMBDA_INIT)
    gl = jnp.dot(xb, wg_ref[...], preferred_element_type=F32) + bg_ref[...]
    gate_a = jax.nn.sigmoid(gl[:, :D_MODEL])
    gate_b = jax.nn.sigmoid(gl[:, D_MODEL:])

    pa = _dot_t(oa_ref[0].reshape(COLS_A_Q, tm), wpa_ref[...])
    ybt = []
    for h in range(B_HEADS):
        y = ob_ref[0, 2 * h].astype(F32) - lam * ob_ref[0, 2 * h + 1].astype(F32)
        y = y * lax.rsqrt(jnp.mean(y * y, axis=0, keepdims=True) + EPS) * sub_ref[...]
        ybt.append((y * (1.0 - LAMBDA_INIT)).astype(BF16))
    pb = _dot_t(jnp.concatenate(ybt, axis=0), wpb_ref[...])
    merged = gate_a * pa + gate_b * pb
    pre = DN_ALPHA * x + jnp.dot(merged.astype(BF16), wo_ref[...], preferred_element_type=F32)
    h_out = _layer_norm(pre, g_ref[...], b_ref[...])
    h_ref[0] = h_out
    ht_ref[...] = h_out.T.astype(ht_ref.dtype)


def _merge(x, oat, obt, wg, bg, wpa, wpb, wo, sub, lams, g, b, tm):
    bsz, seq, _ = x.shape
    nblk = seq // tm
    const = lambda bb, i: (0, 0)
    vec = lambda w: pl.BlockSpec((1, w), const)
    return pl.pallas_call(
        _merge_kernel,
        grid=(bsz, nblk),
        in_specs=[
            pl.BlockSpec((1, tm, D_MODEL), lambda bb, i: (bb, i, 0)),
            pl.BlockSpec((1, A_Q_HEADS, HEAD_DIM, tm), lambda bb, i: (bb, 0, 0, i)),
            pl.BlockSpec((1, 2 * B_HEADS, B_V_DIM, tm), lambda bb, i: (bb, 0, 0, i)),
            pl.BlockSpec((D_MODEL, 2 * D_MODEL), const),
            vec(2 * D_MODEL),
            pl.BlockSpec((COLS_A_Q, D_MODEL), const),
            pl.BlockSpec((COLS_B_V, D_MODEL), const),
            pl.BlockSpec((D_MODEL, D_MODEL), const),
            pl.BlockSpec((B_V_DIM, 1), const),
            vec(HEAD_DIM), vec(HEAD_DIM), vec(HEAD_DIM), vec(HEAD_DIM),
            vec(D_MODEL), vec(D_MODEL),
        ],
        out_specs=(
            pl.BlockSpec((1, tm, D_MODEL), lambda bb, i: (bb, i, 0)),
            pl.BlockSpec((D_MODEL, tm), lambda bb, i: (0, bb * nblk + i)),
        ),
        out_shape=(
            jax.ShapeDtypeStruct((bsz, seq, D_MODEL), F32),
            jax.ShapeDtypeStruct((D_MODEL, bsz * seq), BF16),
        ),
        compiler_params=_cparams(("parallel", "parallel")),
        name="merge",
    )(x, oat, obt, wg, bg, wpa, wpb, wo, sub, *lams, g, b)


_NO_RANK = 32.0


SUBLANES = 8


def _sort_network(n):
    pairs, p = [], 1
    while p < n:
        k = p
        while k >= 1:
            for j in range(k % p, n - k, 2 * k):
                for i in range(min(k, n - j - k)):
                    if (i + j) // (2 * p) == (i + j + k) // (2 * p):
                        pairs.append((i + j, i + j + k))
            k //= 2
        p *= 2
    return pairs


def _top_values(s, n):
    vals = []
    if s.shape[0] != PEER_N_KEYS:
        for _ in range(n):
            m = jnp.max(s, axis=0, keepdims=True)
            vals.append(m)
            s = jnp.where(s == m, -jnp.inf, s)
        return vals
    rows = [s[r * SUBLANES:(r + 1) * SUBLANES] for r in range(s.shape[0] // SUBLANES)]
    for a, b in _sort_network(len(rows)):
        rows[a], rows[b] = jnp.maximum(rows[a], rows[b]), jnp.minimum(rows[a], rows[b])
    for i in range(n):
        m = jnp.max(rows[0], axis=0, keepdims=True)
        vals.append(m)
        hit = rows[0] == m
        depth = min(len(rows), n - i)
        below = rows[1:depth + 1]
        below = below + [jnp.full_like(rows[0], -jnp.inf)] * (depth - len(below))
        rows = [jnp.where(hit, below[k], rows[k]) for k in range(depth)]
    return vals


def _rank_codes(s, vals):
    rank = jnp.full(s.shape, _NO_RANK, F32)
    for i, v in enumerate(vals):
        rank = jnp.where(s == v, float(i), rank)
    return rank


def _route_kernel(ht_ref, wq_ref, keys_ref, cnt_ref, coef_ref, rank_ref, e2_ref):
    qt = jnp.dot(wq_ref[...], ht_ref[...], preferred_element_type=F32).astype(BF16)
    s1_all = jnp.dot(keys_ref[0, 0], qt[:PEER_HALF], preferred_element_type=F32)
    s2_all = jnp.dot(keys_ref[0, 1], qt[PEER_HALF:], preferred_element_type=F32)
    for g in range(s1_all.shape[1] // LANES):
        lanes = slice(g * LANES, (g + 1) * LANES)
        s1 = s1_all[:, lanes]
        s2 = s2_all[:, lanes]
        v1 = _top_values(s1, PEER_TOPK + 1)
        v2 = _top_values(s2, PEER_TOPK + 1)
        rank2 = _rank_codes(s2, v2[:PEER_TOPK])
        pair = {(k, l): v1[k] + v2[l] for k, l in _CAND_PAIRS}
        top = _top_values(jnp.concatenate([pair[p] for p in _CAND_PAIRS], axis=0), PEER_TOPK + 1)
        thr = 0.5 * (top[PEER_TOPK - 1] + top[PEER_TOPK])
        ev1 = [jnp.exp(v - v1[0]) for v in v1]
        ev2 = [jnp.exp(v - v2[0]) for v in v2]
        z = jnp.zeros_like(thr)
        pairs = [jnp.zeros_like(thr) for _ in v1]
        for k, l in _CAND_PAIRS:
            sel = pair[(k, l)] > thr
            z = z + jnp.where(sel, ev1[k] * ev2[l], 0.0)
            pairs[k] = pairs[k] + jnp.where(sel, 1.0, 0.0)
        cnt = jnp.zeros_like(s1)
        for k, v in enumerate(v1[:PEER_TOPK]):
            cnt = jnp.where(s1 == v, pairs[k], cnt)
        cnt_ref[0, :, lanes] = cnt
        coef_ref[0, :, lanes] = jnp.exp(s1 - v1[0]) / z
        rank_ref[0, :, lanes] = rank2.astype(rank_ref.dtype)
        e2_ref[0, :, lanes] = jnp.exp(s2 - v2[0]).astype(e2_ref.dtype)


def _route(ht, wq_t, keys, tm):
    ntok = ht.shape[1]
    shape = (PEER_HEADS, PEER_N_KEYS, ntok)
    ospec = pl.BlockSpec((1, PEER_N_KEYS, tm), lambda t, h: (h, 0, t))
    return pl.pallas_call(
        _route_kernel,
        grid=(ntok // tm, PEER_HEADS),
        in_specs=[
            pl.BlockSpec((D_MODEL, tm), lambda t, h: (0, t)),
            pl.BlockSpec((2 * PEER_HALF, D_MODEL), lambda t, h: (h, 0)),
            pl.BlockSpec((1, 2, PEER_N_KEYS, PEER_HALF), lambda t, h: (h, 0, 0, 0)),
        ],
        out_specs=(ospec, ospec, ospec, ospec),
        out_shape=(jax.ShapeDtypeStruct(shape, F32), jax.ShapeDtypeStruct(shape, F32),
                   jax.ShapeDtypeStruct(shape, BF16), jax.ShapeDtypeStruct(shape, BF16)),
        compiler_params=_cparams(("parallel", "parallel")),
        name="route",
    )(ht, wq_t, keys)


def _gelu(a):
    return 0.5 * a * (1.0 + lax.erf(a * (2.0 ** -0.5)))


def _peer_kernel(ht_ref, u_ref, vt_ref, cnt_ref, coef_ref, rank_ref, e2_ref, h_ref, g_ref, b_ref,
                 o_ref, yt_ref, *, rows_per_step):
    e = pl.program_id(1)

    @pl.when(e == 0)
    def _():
        yt_ref[...] = jnp.zeros_like(yt_ref)

    at = jnp.dot(u_ref[...], ht_ref[...], preferred_element_type=F32)
    tm = at.shape[1]
    pieces = []
    for r in range(rows_per_step):
        gate = jnp.zeros((PEER_N_KEYS, tm), BF16)
        for h in range(PEER_HEADS):
            cnt = jnp.broadcast_to(cnt_ref[h, r:r + 1, :], (BF16_ROWS, tm)).astype(BF16)
            coef = jnp.broadcast_to(coef_ref[h, r:r + 1, :], (BF16_ROWS, tm)).astype(BF16)
            cnt = jnp.tile(cnt, (PEER_N_KEYS // BF16_ROWS, 1))
            coef = jnp.tile(coef, (PEER_N_KEYS // BF16_ROWS, 1))
            gate = gate + jnp.where(rank_ref[h] < cnt, e2_ref[h] * coef, jnp.zeros_like(coef))
        a = at[r * PEER_N_KEYS:(r + 1) * PEER_N_KEYS, :]
        pieces.append(_gelu(a.astype(BF16)) * gate)
    wt = jnp.concatenate(pieces, axis=0)
    yt_ref[:, :tm] += jnp.dot(vt_ref[...], wt, preferred_element_type=F32)

    @pl.when(e == pl.num_programs(1) - 1)
    def _():
        pre = DN_ALPHA * h_ref[...] + yt_ref[:, :tm].T
        o_ref[...] = _layer_norm(pre, g_ref[...], b_ref[...])


def _peer(ht, u, vt, cnt, coef, rank, e2, h, g, b, tm, eb):
    ntok = ht.shape[1]
    rows = eb // PEER_N_KEYS
    full = pl.BlockSpec((PEER_HEADS, PEER_N_KEYS, tm), lambda t, e: (0, 0, t))
    part = pl.BlockSpec((PEER_HEADS, rows, tm), lambda t, e: (0, e, t))
    vec = pl.BlockSpec((1, D_MODEL), lambda t, e: (0, 0))
    return pl.pallas_call(
        functools.partial(_peer_kernel, rows_per_step=rows),
        grid=(ntok // tm, PEER_N_EXPERTS // eb),
        in_specs=[
            pl.BlockSpec((D_MODEL, tm), lambda t, e: (0, t)),
            pl.BlockSpec((eb, D_MODEL), lambda t, e: (e, 0)),
            pl.BlockSpec((D_MODEL, eb), lambda t, e: (0, e)),
            part, part, full, full,
            pl.BlockSpec((tm, D_MODEL), lambda t, e: (t, 0)),
            vec, vec,
        ],
        out_specs=pl.BlockSpec((tm, D_MODEL), lambda t, e: (t, 0)),
        out_shape=jax.ShapeDtypeStruct((ntok, D_MODEL), F32),
        scratch_shapes=[pltpu.VMEM((D_MODEL, tm + LANES), F32)],
        compiler_params=_cparams(("parallel", "arbitrary")),
        name="peer",
    )(ht, u, vt, cnt, coef, rank, e2, h, g, b)


def _hybrid_block(x, w_in, b_gate, q_norm_a, k_norm_a, lambda_q1, lambda_k1, lambda_q2, lambda_k2,
                  subln_b, w_proj_a, w_proj_b, w_out, ln1_g, ln1_b, w_query, sub_keys,
                  u_table, v_table, ln2_g, ln2_b, *, tk, tq_a, tq_b, tm_merge, tm_route, tm_peer, eb):
    bsz, seq, _ = x.shape
    row = lambda a: a.reshape(1, -1).astype(F32)
    two = lambda a: jnp.concatenate([row(a), row(a)], axis=-1)

    qat, ka, vat, qbt, kb, vbt, u_bf, vt_bf = _prep(x, w_in[:, :COLS_QKV].astype(BF16), two(q_norm_a),
                                                    two(k_norm_a), _rope_tables(seq), u_table, v_table, tk)

    nchunk = seq // tk
    oat = _flash(qat.reshape(bsz * A_KV_HEADS, A_GROUP, HEAD_DIM, seq),
                 ka.reshape(bsz * A_KV_HEADS, 1, seq, HEAD_DIM),
                 vat.reshape(bsz * A_KV_HEADS, 1, nchunk, HEAD_DIM + BF16_ROWS, tk),
                 v_of=lambda n: n, tq=tq_a, dv=HEAD_DIM)
    obt = _flash(qbt.reshape(bsz * 2 * B_HEADS, 1, HEAD_DIM, seq),
                 kb.reshape(bsz * 2 * B_HEADS, 1, seq, HEAD_DIM),
                 vbt.reshape(bsz * B_HEADS, 1, nchunk, B_V_DIM + BF16_ROWS, tk),
                 v_of=lambda n: n // 2, tq=tq_b, dv=B_V_DIM)

    h, ht = _merge(x, oat.reshape(bsz, A_Q_HEADS, HEAD_DIM, seq), obt.reshape(bsz, 2 * B_HEADS, B_V_DIM, seq),
                   w_in[:, COLS_QKV:].astype(BF16), row(b_gate), w_proj_a.astype(BF16),
                   w_proj_b.astype(BF16), w_out.astype(BF16), subln_b.reshape(-1, 1).astype(F32),
                   (row(lambda_q1), row(lambda_k1), row(lambda_q2), row(lambda_k2)),
                   row(ln1_g), row(ln1_b), tm_merge)

    cnt, coef, rank, e2 = _route(ht, w_query.T.astype(BF16), sub_keys.astype(BF16), tm_route)
    out = _peer(ht, u_bf, vt_bf, cnt, coef, rank, e2,
                h.reshape(bsz * seq, D_MODEL), row(ln2_g), row(ln2_b), tm_peer, eb)
    return out.reshape(bsz, seq, D_MODEL)


def kernel(x, w_in, b_gate, q_norm_a, k_norm_a, lambda_q1, lambda_k1, lambda_q2, lambda_k2, subln_b,
           w_proj_a, w_proj_b, w_out, ln1_g, ln1_b, w_query, sub_keys, u_table, v_table, ln2_g, ln2_b):
    return _hybrid_block(
        x, w_in[0], b_gate[0], q_norm_a[0], k_norm_a[0], lambda_q1[0], lambda_k1[0], lambda_q2[0],
        lambda_k2[0], subln_b[0], w_proj_a[0], w_proj_b[0], w_out[0], ln1_g[0], ln1_b[0], w_query[0],
        sub_keys[0], u_table[0], v_table[0], ln2_g[0], ln2_b[0],
        tk=512, tq_a=256, tq_b=1024, tm_merge=512, tm_route=1024, tm_peer=512, eb=2048)
```
